```python
import functools
import jax, jax.numpy as jnp
from jax import lax
import numpy as np

D_MODEL = 2048
BATCH = 16
SEQ = 2048
DEPTH = 4
DEC_BATCH = 32
DEC_SEQ = 64
PAST_LEN = 2048

CHUNK = 64
N_A = (DEPTH + 1) // 2
N_C = DEPTH // 2
EPS = 1e-6
RET_HEADS = 8
RET_DK = 128
RET_DV = 128
RET_WIDTH = RET_HEADS * RET_DV
POOL_WINDOWS = (2, 4, 8, 16)
POOL_GROUPS = 4
POOL_WIDTH = D_MODEL // 2
POOL_GROUP_DIM = POOL_WIDTH // POOL_GROUPS
POOL_BUF = max(POOL_WINDOWS) - 1
IN_A = 2 * RET_HEADS * RET_DK + 2 * RET_WIDTH + POOL_WIDTH
ML_HEADS = 8
ML_DK = 128
ML_DV = 256
ML_QK = ML_HEADS * ML_DK
ML_V = ML_HEADS * ML_DV
ML_CONV = 4
IN_C = 2 * ML_QK + 2 * ML_V + 2 * ML_HEADS
D_FF = 5632
FFN_CONV = 3

kernel_name = 'streaming_retention_pool_mlstm_convffn'


def rms_norm(x, g):
    xf = x.astype(jnp.float32)
    y = xf * lax.rsqrt(jnp.mean(xf * xf, -1, keepdims=True) + EPS)
    return (y * g.astype(jnp.float32)).astype(x.dtype)


def head_rms(x):
    return x * lax.rsqrt(jnp.mean(x * x, -1, keepdims=True) + EPS)


def rotary(x, pos):
    half = x.shape[-1] // 2
    inv = 1.0 / (10000.0 ** jnp.linspace(0.0, 1.0, half, dtype=jnp.float32))
    ang = pos.astype(jnp.float32)[:, None] * inv[None, :]
    cos = jnp.cos(ang)[None, :, None, :]
    sin = jnp.sin(ang)[None, :, None, :]
    xf = x.astype(jnp.float32)
    x1, x2 = xf[..., :half], xf[..., half:]
    return jnp.concatenate([x1 * cos - x2 * sin, x1 * sin + x2 * cos], -1)


def causal_dwconv(ext, w, L):
    acc = ext[:, 0:L] * w[0]
    for j in range(1, w.shape[0]):
        acc = acc + ext[:, j:j + L] * w[j]
    return acc


def run_chunks(step, carry, seqs):
    L = seqs[0].shape[1]
    if L <= CHUNK:
        return step(carry, *seqs)
    n = L // CHUNK
    split = lambda a: jnp.moveaxis(a.reshape(a.shape[0], n, CHUNK, *a.shape[2:]), 1, 0)
    carry, ys = lax.scan(lambda c, xs: step(c, *xs), carry, tuple(split(a) for a in seqs))
    ys = jnp.moveaxis(ys, 0, 1)
    return carry, ys.reshape(ys.shape[0], L, *ys.shape[3:])


def retention_chunk(S, q, k, v, log_gamma):
    L = q.shape[1]
    idx = jnp.arange(L, dtype=jnp.float32)
    diff = idx[:, None] - idx[None, :]
    dec = jnp.where(diff >= 0, jnp.exp(log_gamma[:, None, None] * jnp.maximum(diff, 0.0)), 0.0)
    s = jnp.einsum('blhd,bmhd->bhlm', q, k) * dec
    intra = jnp.einsum('bhlm,bmhe->blhe', s, v)
    q_dec = jnp.exp((idx[:, None] + 1.0) * log_gamma)
    inter = jnp.einsum('blhd,bhde->blhe', q, S) * q_dec[None, :, :, None]
    k_dec = jnp.exp((L - 1.0 - idx[:, None]) * log_gamma)
    S_new = (jnp.exp(L * log_gamma)[None, :, None, None] * S
             + jnp.einsum('blhd,blhe->bhde', k * k_dec[None, :, :, None], v))
    return S_new, intra + inter


def mlstm_chunk(carry, q, k, v, ig, lf):
    C, n, m = carry
    L = q.shape[1]
    causal = jnp.tril(jnp.ones((L, L), bool))
    b = jnp.moveaxis(jnp.cumsum(lf, 1), 1, 2)
    i = jnp.moveaxis(ig, 1, 2)
    dlog = jnp.where(causal, b[..., :, None] - b[..., None, :] + i[..., None, :], -jnp.inf)
    inter_log = b + m[..., None]
    m_t = jnp.maximum(inter_log, jnp.max(dlog, -1))
    dw = jnp.exp(dlog - m_t[..., None])
    iw = jnp.exp(inter_log - m_t)
    s = jnp.einsum('blhd,bmhd->bhlm', q, k) * dw
    iw_t = jnp.moveaxis(iw, 1, 2)[..., None]
    num = jnp.einsum('bhlm,bmhe->blhe', s, v) + jnp.einsum('blhd,bhde->blhe', q, C) * iw_t
    den = jnp.sum(s, -1) + jnp.einsum('blhd,bhd->bhl', q, n) * iw
    floor = jnp.maximum(jnp.abs(den), jnp.exp(-m_t))
    h = num / jnp.moveaxis(floor, 1, 2)[..., None]
    m_L = m_t[..., -1]
    w_end = jnp.exp(b[..., -1:] - b + i - m_L[..., None])
    decay = jnp.exp(b[..., -1] + m - m_L)
    C_new = decay[..., None, None] * C + jnp.einsum('bhs,bshd,bshe->bhde', w_end, k, v)
    n_new = decay[..., None] * n + jnp.einsum('bhs,bshd->bhd', w_end, k)
    return (C_new, n_new, m_L), h


def multiscale_pool(u, buf, pos):
    Bn, L, P = u.shape
    ext = jnp.concatenate([buf.astype(u.dtype), u], 1)
    cz = jnp.concatenate([jnp.zeros((Bn, 1, P), jnp.float32), jnp.cumsum(ext.astype(jnp.float32), 1)], 1)
    cz = cz.reshape(Bn, POOL_BUF + L + 1, POOL_GROUPS, POOL_GROUP_DIM)
    end = cz[:, POOL_BUF + 1:]
    means = []
    for j, w in enumerate(POOL_WINDOWS):
        start = cz[:, POOL_BUF + 1 - w:POOL_BUF + 1 - w + L, j]
        cnt = jnp.minimum(pos + 1, w).astype(jnp.float32)[None, :, None]
        means.append((end[:, :, j] - start) / cnt)
    mean = jnp.stack(means, 2)
    pooled = mean - u.reshape(Bn, L, POOL_GROUPS, POOL_GROUP_DIM).astype(jnp.float32)
    return pooled, ext[:, -POOL_BUF:]


def retention_pool_mixer(h, pos, S0, pool_buf, w_in, w_pool, pool_scale, w_out):
    Bn, L, _ = h.shape
    proj = h @ w_in
    qw = RET_HEADS * RET_DK
    q = proj[..., :qw].reshape(Bn, L, RET_HEADS, RET_DK)
    k = proj[..., qw:2 * qw].reshape(Bn, L, RET_HEADS, RET_DK)
    v = proj[..., 2 * qw:2 * qw + RET_WIDTH].reshape(Bn, L, RET_HEADS, RET_DV).astype(jnp.float32)
    g = proj[..., 2 * qw + RET_WIDTH:2 * qw + 2 * RET_WIDTH].astype(jnp.float32)
    u = proj[..., 2 * qw + 2 * RET_WIDTH:]
    q = rotary(q, pos)
    k = rotary(k, pos) * (RET_DK ** -0.5)
    log_gamma = jnp.log1p(-(2.0 ** (-5.0 - jnp.arange(RET_HEADS, dtype=jnp.float32))))
    step = functools.partial(retention_chunk, log_gamma=log_gamma)
    S, o = run_chunks(step, S0.astype(jnp.float32), (q, k, v))
    ret = head_rms(o).reshape(Bn, L, RET_WIDTH) * jax.nn.silu(g)
    pooled, new_buf = multiscale_pool(u, pool_buf, pos)
    pool = jnp.einsum('blgc,gcd->blgd', pooled, w_pool).reshape(Bn, L, POOL_WIDTH) * pool_scale
    y = jnp.concatenate([ret.astype(h.dtype), pool.astype(h.dtype)], -1) @ w_out
    return y, S, new_buf


def mlstm_mixer(h, conv_buf, C0, n0, m0, w_in, conv_w, conv_b, b_i, b_f, norm_g, w_out):
    Bn, L, _ = h.shape
    proj = h @ w_in
    qk_pre = proj[..., :2 * ML_QK]
    v = proj[..., 2 * ML_QK:2 * ML_QK + ML_V].reshape(Bn, L, ML_HEADS, ML_DV).astype(jnp.float32)
    o = proj[..., 2 * ML_QK + ML_V:2 * ML_QK + 2 * ML_V].astype(jnp.float32)
    ig = proj[..., 2 * ML_QK + 2 * ML_V:2 * ML_QK + 2 * ML_V + ML_HEADS].astype(jnp.float32) + b_i
    fg = proj[..., 2 * ML_QK + 2 * ML_V + ML_HEADS:].astype(jnp.float32) + b_f
    ext = jnp.concatenate([conv_buf.astype(qk_pre.dtype), qk_pre], 1)
    qk = jax.nn.silu((causal_dwconv(ext, conv_w, L) + conv_b).astype(jnp.float32))
    q = qk[..., :ML_QK].reshape(Bn, L, ML_HEADS, ML_DK)
    k = qk[..., ML_QK:].reshape(Bn, L, ML_HEADS, ML_DK) * (ML_DK ** -0.5)
    lf = jax.nn.log_sigmoid(fg)
    carry0 = (C0.astype(jnp.float32), n0.astype(jnp.float32), m0.astype(jnp.float32))
    (C, n, m), hh = run_chunks(mlstm_chunk, carry0, (q, k, v, ig, lf))
    hh = head_rms(hh) * norm_g.astype(jnp.float32).reshape(ML_HEADS, ML_DV)
    y = (hh.reshape(Bn, L, ML_V) * jax.nn.sigmoid(o)).astype(h.dtype) @ w_out
    return y, C, n, m, ext[:, -(ML_CONV - 1):]


def conv_ffn(h, buf, w_up, w_gate, conv_w, conv_b, w_down):
    L = h.shape[1]
    up = h @ w_up
    gate = h @ w_gate
    ext = jnp.concatenate([buf.astype(gate.dtype), gate], 1)
    act = jax.nn.gelu(causal_dwconv(ext, conv_w, L) + conv_b)
    return (act * up) @ w_down, ext[:, -(FFN_CONV - 1):]


def trunk(x, pos, ret_S, pool_buf, ml_C, ml_n, ml_m, ml_conv, ffn_buf, p):
    o_S, o_pool, o_C, o_n, o_m, o_conv, o_ffn = [], [], [], [], [], [], []
    for l in range(DEPTH):
        h = rms_norm(x, p['norm_mix'][l])
        if l % 2 == 0:
            a = l // 2
            y, S, pb = retention_pool_mixer(h, pos, ret_S[a], pool_buf[a], p['w_in_a'][a], p['w_pool'][a],
                                            p['pool_scale'][a], p['w_out_a'][a])
            o_S.append(S)
            o_pool.append(pb)
        else:
            c = l // 2
            y, C, n, m, cb = mlstm_mixer(h, ml_conv[c], ml_C[c], ml_n[c], ml_m[c], p['w_in_c'][c],
                                         p['conv_c_w'][c], p['conv_c_b'][c], p['b_i'][c], p['b_f'][c],
                                         p['ml_norm'][c], p['w_out_c'][c])
            o_C.append(C)
            o_n.append(n)
            o_m.append(m)
            o_conv.append(cb)
        x = x + y
        y, fb = conv_ffn(rms_norm(x, p['norm_ffn'][l]), ffn_buf[l], p['w_up'][l], p['w_gate'][l],
                         p['ffn_conv_w'][l], p['ffn_conv_b'][l], p['w_down'][l])
        o_ffn.append(fb)
        x = x + y
    return (rms_norm(x, p['norm_final']), jnp.stack(o_S), jnp.stack(o_pool), jnp.stack(o_C),
            jnp.stack(o_n), jnp.stack(o_m), jnp.stack(o_conv), jnp.stack(o_ffn))


def setup_inputs(seed: int = 0) -> dict:
    key = jax.random.key(seed)
    ks = iter(jax.random.split(key, 40))
    nrm = lambda shape, scale: scale * jax.random.normal(next(ks), shape, jnp.float32)
    return {
        'x_prompt': nrm((BATCH, SEQ, D_MODEL), 1.0),
        'x_sample': nrm((DEC_BATCH, DEC_SEQ, D_MODEL), 1.0),
        'state_ret': nrm((N_A, DEC_BATCH, RET_HEADS, RET_DK, RET_DV), 0.3),
        'cache_pool': nrm((N_A, DEC_BATCH, POOL_BUF, POOL_WIDTH), 1.0),
        'state_mlstm_C': nrm((N_C, DEC_BATCH, ML_HEADS, ML_DK, ML_DV), 0.3),
        'state_mlstm_n': nrm((N_C, DEC_BATCH, ML_HEADS, ML_DK), 0.3),
        'state_mlstm_m': nrm((N_C, DEC_BATCH, ML_HEADS), 1.0),
        'cache_mlstm_conv': nrm((N_C, DEC_BATCH, ML_CONV - 1, 2 * ML_QK), 1.0),
        'cache_ffn_conv': nrm((DEPTH, DEC_BATCH, FFN_CONV - 1, D_FF), 1.0),
        'norm_mix': 1.0 + nrm((DEPTH, D_MODEL), 0.02),
        'norm_ffn': 1.0 + nrm((DEPTH, D_MODEL), 0.02),
        'norm_final': 1.0 + nrm((D_MODEL,), 0.02),
        'w_in_a': nrm((N_A, D_MODEL, IN_A), D_MODEL ** -0.5),
        'w_pool': nrm((N_A, POOL_GROUPS, POOL_GROUP_DIM, POOL_GROUP_DIM), POOL_GROUP_DIM ** -0.5),
        'pool_scale': 1.0 + nrm((N_A, POOL_WIDTH), 0.1),
        'w_out_a': nrm((N_A, RET_WIDTH + POOL_WIDTH, D_MODEL), (RET_WIDTH + POOL_WIDTH) ** -0.5),
        'w_in_c': nrm((N_C, D_MODEL, IN_C), D_MODEL ** -0.5),
        'conv_c_w': nrm((N_C, ML_CONV, 2 * ML_QK), 0.5),
        'conv_c_b': nrm((N_C, 2 * ML_QK), 0.01),
        'b_i': nrm((N_C, ML_HEADS), 0.1),
        'b_f': jnp.linspace(3.0, 6.0, ML_HEADS, dtype=jnp.float32)[None, :] + nrm((N_C, ML_HEADS), 0.1),
        'ml_norm': 1.0 + nrm((N_C, ML_V), 0.02),
        'w_out_c': nrm((N_C, ML_V, D_MODEL), ML_V ** -0.5),
        'w_up': nrm((DEPTH, D_MODEL, D_FF), D_MODEL ** -0.5),
        'w_gate': nrm((DEPTH, D_MODEL, D_FF), D_MODEL ** -0.5),
        'ffn_conv_w': nrm((DEPTH, FFN_CONV, D_FF), 0.6),
        'ffn_conv_b': nrm((DEPTH, D_FF), 0.01),
        'w_down': nrm((DEPTH, D_FF, D_MODEL), D_FF ** -0.5),
    }


def reference(x_prompt, x_sample, state_ret, cache_pool, state_mlstm_C, state_mlstm_n, state_mlstm_m,
              cache_mlstm_conv, cache_ffn_conv, norm_mix, norm_ffn, norm_final, w_in_a, w_pool, pool_scale,
              w_out_a, w_in_c, conv_c_w, conv_c_b, b_i, b_f, ml_norm, w_out_c, w_up, w_gate, ffn_conv_w,
              ffn_conv_b, w_down):
    p = dict(norm_mix=norm_mix, norm_ffn=norm_ffn, norm_final=norm_final, w_in_a=w_in_a, w_pool=w_pool,
             pool_scale=pool_scale, w_out_a=w_out_a, w_in_c=w_in_c, conv_c_w=conv_c_w, conv_c_b=conv_c_b,
             b_i=b_i, b_f=b_f, ml_norm=ml_norm, w_out_c=w_out_c, w_up=w_up, w_gate=w_gate,
             ffn_conv_w=ffn_conv_w, ffn_conv_b=ffn_conv_b, w_down=w_down)
    f32 = jnp.float32
    B0, T0 = x_prompt.shape[0], x_prompt.shape[1]
    dt = x_prompt.dtype
    pos_p = jnp.arange(T0, dtype=jnp.int32)
    (y_prompt, p_ret, p_pool, p_C, p_n, p_m, p_conv, p_ffn) = trunk(
        x_prompt, pos_p,
        jnp.zeros((N_A, B0, RET_HEADS, RET_DK, RET_DV), f32),
        jnp.zeros((N_A, B0, POOL_BUF, POOL_WIDTH), dt),
        jnp.zeros((N_C, B0, ML_HEADS, ML_DK, ML_DV), f32),
        jnp.zeros((N_C, B0, ML_HEADS, ML_DK), f32),
        jnp.zeros((N_C, B0, ML_HEADS), f32),
        jnp.zeros((N_C, B0, ML_CONV - 1, 2 * ML_QK), dt),
        jnp.zeros((DEPTH, B0, FFN_CONV - 1, D_FF), dt), p)
    pos_s = PAST_LEN + jnp.arange(x_sample.shape[1], dtype=jnp.int32)
    (y_sample, s_ret, s_pool, s_C, s_n, s_m, s_conv, s_ffn) = trunk(
        x_sample, pos_s, state_ret, cache_pool, state_mlstm_C, state_mlstm_n, state_mlstm_m,
        cache_mlstm_conv, cache_ffn_conv, p)
    return (y_prompt, y_sample, p_ret, p_pool, p_C, p_n, p_m, p_conv, p_ffn,
            s_ret, s_pool, s_C, s_n, s_m, s_conv, s_ffn)
```

```python
import functools
import math

import jax
import jax.numpy as jnp
from jax import lax
from jax.experimental import pallas as pl
from jax.experimental.pallas import tpu as pltpu

EPS = 1e-6
PAST_LEN = 2048
POOL_WINDOWS = (2, 4, 8, 16)
F32 = jnp.float32
BF16 = jnp.bfloat16

SUBLANES = 8
LANES = 128
POOL_HALO = 16
VMEM_LIMIT = 56 * 1024 * 1024

ROW_TILE = 512
PROJ_COLS = 256
OUT_COLS = 512
FFN_COLS = 512
RET_CHUNK = 256
ML_CHUNK = 256
POOL_ROWS = 512


def _params(n_axes):
    return pltpu.CompilerParams(dimension_semantics=("arbitrary",) * n_axes, vmem_limit_bytes=VMEM_LIMIT)


def _rms(x, g):
    return x * lax.rsqrt(jnp.mean(x * x, -1, keepdims=True) + EPS) * g


def _dot(a, b):
    return jnp.dot(a, b, preferred_element_type=F32)


def _dot_nt(a, b):
    return lax.dot_general(a, b, (((1,), (1,)), ((), ())), preferred_element_type=F32)


def _dot_tn(a, b):
    return lax.dot_general(a, b, (((0,), (0,)), ((), ())), preferred_element_type=F32)


def _proj_kernel(*refs, n_seg, has_small):
    x_ref, g_ref = refs[0], refs[1]
    k = 2
    w_refs = refs[k:k + n_seg]
    k += n_seg
    ws_ref = refs[k] if has_small else None
    k += has_small
    o_refs = refs[k:k + n_seg]
    k += n_seg
    os_ref = refs[k] if has_small else None
    k += has_small
    h_ref = refs[k]

    @pl.when(pl.program_id(1) == 0)
    def _():
        h_ref[...] = _rms(x_ref[...], g_ref[...]).astype(BF16)
        if has_small:
            os_ref[...] = _dot(h_ref[...], ws_ref[...])

    h = h_ref[...]
    for w_ref, o_ref in zip(w_refs, o_refs):
        o_ref[...] = _dot(h, w_ref[...]).astype(o_ref.dtype)


def _proj(x, g, ws, out_dtypes, w_small=None):
    M, D = x.shape
    n_seg = len(ws)
    wseg = ws[0].shape[1]
    tm, tn = min(ROW_TILE, M), min(PROJ_COLS, wseg)
    assert M % tm == 0 and wseg % tn == 0
    has_small = w_small is not None
    in_specs = [pl.BlockSpec((tm, D), lambda i, j: (i, 0)), pl.BlockSpec((1, D), lambda i, j: (0, 0))]
    in_specs += [pl.BlockSpec((D, tn), lambda i, j: (0, j)) for _ in ws]
    out_specs = [pl.BlockSpec((tm, tn), lambda i, j: (i, j)) for _ in ws]
    out_shape = [jax.ShapeDtypeStruct((M, wseg), dt) for dt in out_dtypes]
    args = [x, g.reshape(1, D)] + list(ws)
    if has_small:
        ns = w_small.shape[1]
        in_specs.append(pl.BlockSpec((D, ns), lambda i, j: (0, 0)))
        out_specs.append(pl.BlockSpec((tm, ns), lambda i, j: (i, 0)))
        out_shape.append(jax.ShapeDtypeStruct((M, ns), F32))
        args.append(w_small)
    return pl.pallas_call(
        functools.partial(_proj_kernel, n_seg=n_seg, has_small=has_small),
        grid=(M // tm, wseg // tn),
        in_specs=in_specs, out_specs=out_specs, out_shape=out_shape,
        scratch_shapes=[pltpu.VMEM((tm, D), BF16)],
        compiler_params=_params(2), name="norm_proj",
    )(*args)


def _outproj_kernel(*refs, n_in):
    a_refs = refs[:n_in]
    w_ref, x_ref, o_ref = refs[n_in], refs[n_in + 1], refs[n_in + 2]
    acc = x_ref[...]
    off = 0
    for a_ref in a_refs:
        kk = a_ref.shape[1]
        acc = acc + _dot(a_ref[...], w_ref[off:off + kk, :])
        off += kk
    o_ref[...] = acc


def _outproj(acts, w, x):
    M, N = x.shape
    K = w.shape[0]
    tm, tn = min(ROW_TILE, M), min(OUT_COLS, N)
    assert M % tm == 0 and N % tn == 0 and sum(a.shape[1] for a in acts) == K
    in_specs = [pl.BlockSpec((tm, a.shape[1]), lambda i, j: (i, 0)) for a in acts]
    in_specs += [pl.BlockSpec((K, tn), lambda i, j: (0, j)), pl.BlockSpec((tm, tn), lambda i, j: (i, j))]
    return pl.pallas_call(
        functools.partial(_outproj_kernel, n_in=len(acts)),
        grid=(M // tm, N // tn),
        in_specs=in_specs, out_specs=pl.BlockSpec((tm, tn), lambda i, j: (i, j)),
        out_shape=jax.ShapeDtypeStruct((M, N), F32),
        compiler_params=_params(2), name="out_proj",
    )(*acts, w, x)


def _ffn_kernel(x_ref, g_ref, wu_ref, wg_ref, cw_ref, cb_ref, wd_ref, cache_ref, o_ref, nc_ref,
                h_ref, ext_ref, carry_ref, *, S, R, W):
    r, j = pl.program_id(1), pl.program_id(2)
    D = x_ref.shape[-1]
    tf = wu_ref.shape[1]
    H8 = SUBLANES

    @pl.when(j == 0)
    def _():
        x = x_ref[...].reshape(S * R, D)
        h_ref[...] = _rms(x, g_ref[...]).astype(BF16)
        o_ref[...] = x_ref[...]

    h = h_ref[...]
    up = _dot(h, wu_ref[...])
    gate = _dot(h, wg_ref[...])

    @pl.when(r == 0)
    def _():
        ext_ref[:, 0:H8, :] = cache_ref[...]

    @pl.when(r > 0)
    def _():
        ext_ref[:, 0:H8, :] = carry_ref[j]

    ext_ref[:, H8:H8 + R, :] = gate.reshape(S, R, tf)
    acc = None
    for t in range(W):
        term = ext_ref[:, pl.ds(H8 - (W - 1) + t, R), :] * cw_ref[t:t + 1, :]
        acc = term if acc is None else acc + term
    act = jax.nn.gelu(acc + cb_ref[...]) * up.reshape(S, R, tf)
    carry_ref[j] = ext_ref[:, R:R + H8, :]
    nc_ref[:, 0] = ext_ref[:, H8 + R - (W - 1):H8 + R, :]
    y = _dot(act.reshape(S * R, tf).astype(BF16), wd_ref[...])
    o_ref[...] += y.reshape(S, R, D)


def _ffn(x, g, w_up, w_gate, conv_w, conv_b, w_down, cache):
    B, L, D = x.shape
    F = w_up.shape[1]
    W = conv_w.shape[0]
    R = min(ROW_TILE, L)
    S = min(B, max(1, ROW_TILE // R))
    tf = min(FFN_COLS, F)
    assert L % R == 0 and B % S == 0 and F % tf == 0 and R % SUBLANES == 0 and R >= SUBLANES >= W - 1
    nj = F // tf
    cache8 = jnp.pad(cache, ((0, 0), (SUBLANES - (W - 1), 0), (0, 0)))
    y, tails = pl.pallas_call(
        functools.partial(_ffn_kernel, S=S, R=R, W=W),
        grid=(B // S, L // R, nj),
        in_specs=[
            pl.BlockSpec((S, R, D), lambda b, r, j: (b, r, 0)),
            pl.BlockSpec((1, D), lambda b, r, j: (0, 0)),
            pl.BlockSpec((D, tf), lambda b, r, j: (0, j)),
            pl.BlockSpec((D, tf), lambda b, r, j: (0, j)),
            pl.BlockSpec((W, tf), lambda b, r, j: (0, j)),
            pl.BlockSpec((1, tf), lambda b, r, j: (0, j)),
            pl.BlockSpec((tf, D), lambda b, r, j: (j, 0)),
            pl.BlockSpec((S, SUBLANES, tf), lambda b, r, j: (b, 0, j)),
        ],
        out_specs=[
            pl.BlockSpec((S, R, D), lambda b, r, j: (b, r, 0)),
            pl.BlockSpec((S, 1, W - 1, tf), lambda b, r, j: (b, r, 0, j)),
        ],
        out_shape=[jax.ShapeDtypeStruct((B, L, D), F32), jax.ShapeDtypeStruct((B, L // R, W - 1, F), F32)],
        scratch_shapes=[
            pltpu.VMEM((S * R, D), BF16),
            pltpu.VMEM((S, R + SUBLANES, tf), F32),
            pltpu.VMEM((nj, S, SUBLANES, tf), F32),
        ],
        compiler_params=_params(3), name="conv_ffn",
    )(x, g.reshape(1, D), w_up, w_gate, conv_w, conv_b.reshape(1, F), w_down, cache8)
    return y, tails[:, -1]


def _ret_kernel(q_ref, k_ref, v_ref, g_ref, cos_ref, sin_ref, s0_ref, o_ref, so_ref, S_ref, *, H, dk, dv, Lc):
    c = pl.program_id(1)

    @pl.when(c == 0)
    def _():
        S_ref[...] = s0_ref[0]

    idx = lax.broadcasted_iota(jnp.int32, (Lc, 1), 0).astype(F32)
    diff = (lax.broadcasted_iota(jnp.int32, (Lc, Lc), 0) - lax.broadcasted_iota(jnp.int32, (Lc, Lc), 1)).astype(F32)
    cos, sin = cos_ref[...], sin_ref[...]
    for h in range(H):
        lg = math.log1p(-(2.0 ** (-5.0 - h)))
        q = q_ref[0, :, h * dk:(h + 1) * dk].astype(F32)
        k = k_ref[0, :, h * dk:(h + 1) * dk].astype(F32)
        qr = q * cos + pltpu.roll(q, dk // 2, 1) * sin
        kr = (k * cos + pltpu.roll(k, dk // 2, 1) * sin) * (dk ** -0.5)
        v = v_ref[0, :, h * dv:(h + 1) * dv]
        dec = jnp.where(diff >= 0, jnp.exp(lg * jnp.maximum(diff, 0.0)), 0.0)
        qb = qr.astype(BF16)
        s = _dot_nt(qb, kr.astype(BF16)) * dec
        Sh = S_ref[h]
        o = _dot(s.astype(BF16), v) + _dot(qb, Sh.astype(BF16)) * jnp.exp((idx + 1.0) * lg)
        kd = kr * jnp.exp((Lc - 1.0 - idx) * lg)
        S_ref[h] = math.exp(Lc * lg) * Sh + _dot_tn(kd.astype(BF16), v)
        on = o * lax.rsqrt(jnp.mean(o * o, -1, keepdims=True) + EPS)
        gg = g_ref[0, :, h * dv:(h + 1) * dv].astype(F32)
        o_ref[0, :, h * dv:(h + 1) * dv] = (on * (gg * jax.nn.sigmoid(gg))).astype(o_ref.dtype)

    @pl.when(c == pl.num_programs(1) - 1)
    def _():
        so_ref[0] = S_ref[...]


def _retention(q, k, v, g, cos, sin, S0):
    B, L, _ = q.shape
    _, H, dk, dv = S0.shape
    Lc = min(RET_CHUNK, L)
    assert L % Lc == 0
    seq = lambda w: pl.BlockSpec((1, Lc, w), lambda b, c: (b, c, 0))
    tab = pl.BlockSpec((Lc, dk), lambda b, c: (c, 0))
    st = pl.BlockSpec((1, H, dk, dv), lambda b, c: (b, 0, 0, 0))
    return pl.pallas_call(
        functools.partial(_ret_kernel, H=H, dk=dk, dv=dv, Lc=Lc),
        grid=(B, L // Lc),
        in_specs=[seq(H * dk), seq(H * dk), seq(H * dv), seq(H * dv), tab, tab, st],
        out_specs=[seq(H * dv), st],
        out_shape=[jax.ShapeDtypeStruct((B, L, H * dv), BF16), jax.ShapeDtypeStruct(S0.shape, F32)],
        scratch_shapes=[pltpu.VMEM((H, dk, dv), F32)],
        compiler_params=_params(2), name="retention",
    )(q, k, v, g, cos, sin, S0)


def _pool_kernel(u_ref, cache_ref, wp_ref, ps_ref, o_ref, ext_ref, *, tr, gd, pos0):
    r = pl.program_id(1)
    PB = POOL_HALO

    @pl.when(r == 0)
    def _():
        ext_ref[0:PB, :] = cache_ref[0]

    @pl.when(r > 0)
    def _():
        ext_ref[0:PB, :] = ext_ref[tr:tr + PB, :]

    ext_ref[PB:PB + tr, :] = u_ref[0]
    pos = pos0 + r * tr + lax.broadcasted_iota(jnp.int32, (tr, 1), 0)
    for j, w in enumerate(POOL_WINDOWS):
        lo, hi = j * gd, (j + 1) * gd
        acc = ext_ref[PB:PB + tr, lo:hi]
        for d in range(1, w):
            acc = acc + ext_ref[pl.ds(PB - d, tr), lo:hi]
        cnt = jnp.minimum(pos + 1, w).astype(F32)
        pooled = acc / cnt - u_ref[0, :, lo:hi]
        y = _dot(pooled.astype(BF16), wp_ref[j]) * ps_ref[:, lo:hi]
        o_ref[0, :, lo:hi] = y.astype(o_ref.dtype)


def _pool(u, cache, w_pool, pool_scale, pos0):
    B, L, P = u.shape
    G, gd, _ = w_pool.shape
    assert G == len(POOL_WINDOWS) and G * gd == P and cache.shape[1] == max(POOL_WINDOWS) - 1
    tr = min(POOL_ROWS, L)
    assert L % tr == 0 and tr >= POOL_HALO
    cache_h = jnp.pad(cache, ((0, 0), (POOL_HALO - cache.shape[1], 0), (0, 0)))
    return pl.pallas_call(
        functools.partial(_pool_kernel, tr=tr, gd=gd, pos0=pos0),
        grid=(B, L // tr),
        in_specs=[
            pl.BlockSpec((1, tr, P), lambda b, r: (b, r, 0)),
            pl.BlockSpec((1, POOL_HALO, P), lambda b, r: (b, 0, 0)),
            pl.BlockSpec((G, gd, gd), lambda b, r: (0, 0, 0)),
            pl.BlockSpec((1, P), lambda b, r: (0, 0)),
        ],
        out_specs=pl.BlockSpec((1, tr, P), lambda b, r: (b, r, 0)),
        out_shape=jax.ShapeDtypeStruct((B, L, P), BF16),
        scratch_shapes=[pltpu.VMEM((tr + POOL_HALO, P), F32)],
        compiler_params=_params(2), name="ms_pool",
    )(u, cache_h, w_pool, pool_scale.reshape(1, P))


def _mlstm_kernel(qk_ref, v_ref, og_ref, gt_ref, cache_ref, cw_ref, cb_ref, bias_ref, ng_ref,
                  C0_ref, n0_ref, m0_ref, y_ref, Co_ref, no_ref, mo_ref,
                  ext_ref, C_ref, n_ref, m_ref, *, H, dk, dv, Lc, W):
    c = pl.program_id(1)
    H8 = SUBLANES

    @pl.when(c == 0)
    def _():
        ext_ref[0:H8, :] = cache_ref[0]
        C_ref[...] = C0_ref[0]
        n_ref[...] = n0_ref[0]
        m_ref[...] = m0_ref[0]

    @pl.when(c > 0)
    def _():
        ext_ref[0:H8, :] = ext_ref[Lc:Lc + H8, :]

    ext_ref[H8:H8 + Lc, :] = qk_ref[0]

    def conv_silu(lo, hi):
        acc = None
        for t in range(W):
            term = ext_ref[pl.ds(H8 - (W - 1) + t, Lc), lo:hi] * cw_ref[t:t + 1, lo:hi]
            acc = term if acc is None else acc + term
        pre = acc + cb_ref[:, lo:hi]
        return pre * jax.nn.sigmoid(pre)

    gt = gt_ref[0] + bias_ref[...]
    lane = lax.broadcasted_iota(jnp.int32, gt.shape, 1)
    row = lax.broadcasted_iota(jnp.int32, gt.shape, 0)
    b = jax.nn.log_sigmoid(gt)
    sh = 1
    while sh < Lc:
        b = b + jnp.where(row >= sh, pltpu.roll(b, sh, 0), 0.0)
        sh *= 2
    A = jnp.where(lane < H, gt, b)
    AT = A.T
    causal = lax.broadcasted_iota(jnp.int32, (Lc, Lc), 0) >= lax.broadcasted_iota(jnp.int32, (Lc, Lc), 1)

    for h in range(H):
        i_col, b_col = A[:, h:h + 1], A[:, H + h:H + h + 1]
        i_row, b_row = AT[h:h + 1, :], AT[H + h:H + h + 1, :]
        m_prev = m_ref[h:h + 1, 0:1]
        dlog = jnp.where(causal, b_col - b_row + i_row, -jnp.inf)
        inter_log = b_col + m_prev
        m_t = jnp.maximum(inter_log, jnp.max(dlog, -1, keepdims=True))
        dw = jnp.exp(dlog - m_t)
        iw = jnp.exp(inter_log - m_t)
        q = conv_silu(h * dk, (h + 1) * dk)
        k = conv_silu((H + h) * dk, (H + h + 1) * dk) * (dk ** -0.5)
        qb = q.astype(BF16)
        s = _dot_nt(qb, k.astype(BF16)) * dw
        v = v_ref[0, :, h * dv:(h + 1) * dv]
        Ch = C_ref[h]
        nh = n_ref[h:h + 1, :]
        num = _dot(s.astype(BF16), v) + _dot(qb, Ch.astype(BF16)) * iw
        den = jnp.sum(s, -1, keepdims=True) + jnp.sum(q * nh, -1, keepdims=True) * iw
        hh = num / jnp.maximum(jnp.abs(den), jnp.exp(-m_t))
        m_L, b_L = m_t[Lc - 1:Lc, :], b_col[Lc - 1:Lc, :]
        kw = k * jnp.exp(b_L - b_col + i_col - m_L)
        decay = jnp.exp(b_L + m_prev - m_L)
        C_ref[h] = decay * Ch + _dot_tn(kw.astype(BF16), v)
        n_ref[h:h + 1, :] = decay * nh + jnp.sum(kw, 0, keepdims=True)
        m_ref[h:h + 1, :] = jnp.broadcast_to(m_L, (1, m_ref.shape[1]))
        hn = hh * lax.rsqrt(jnp.mean(hh * hh, -1, keepdims=True) + EPS) * ng_ref[:, h * dv:(h + 1) * dv]
        og = og_ref[0, :, h * dv:(h + 1) * dv].astype(F32)
        y_ref[0, :, h * dv:(h + 1) * dv] = (hn * jax.nn.sigmoid(og)).astype(y_ref.dtype)

    @pl.when(c == pl.num_programs(1) - 1)
    def _():
        Co_ref[0] = C_ref[...]
        no_ref[0] = n_ref[...]
        mo_ref[0] = m_ref[...]


def _mlstm(qk, v, og, gt, cache, conv_w, conv_b, gate_bias, norm_g, C0, n0, m0):
    B, L, QK = qk.shape
    _, H, dk, dv = C0.shape
    W = conv_w.shape[0]
    NG = gt.shape[-1]
    Lc = min(ML_CHUNK, L)
    assert L % Lc == 0 and Lc >= SUBLANES >= W - 1 and QK == 2 * H * dk and H == SUBLANES and dk == LANES
    cache8 = jnp.pad(cache, ((0, 0), (SUBLANES - (W - 1), 0), (0, 0)))
    m0b = jnp.broadcast_to(m0[:, :, None], (B, H, LANES))
    seq = lambda w: pl.BlockSpec((1, Lc, w), lambda b, c: (b, c, 0))
    const = lambda shape: pl.BlockSpec(shape, lambda b, c: (0,) * len(shape))
    per_b = lambda shape: pl.BlockSpec((1,) + shape, lambda b, c: (b,) + (0,) * len(shape))
    y, C, n, m = pl.pallas_call(
        functools.partial(_mlstm_kernel, H=H, dk=dk, dv=dv, Lc=Lc, W=W),
        grid=(B, L // Lc),
        in_specs=[seq(QK), seq(H * dv), seq(H * dv), seq(NG), per_b((SUBLANES, QK)),
                  const((W, QK)), const((1, QK)), const((1, NG)), const((1, H * dv)),
                  per_b((H, dk, dv)), per_b((H, dk)), per_b((H, LANES))],
        out_specs=[seq(H * dv), per_b((H, dk, dv)), per_b((H, dk)), per_b((H, LANES))],
        out_shape=[jax.ShapeDtypeStruct((B, L, H * dv), BF16), jax.ShapeDtypeStruct(C0.shape, F32),
                   jax.ShapeDtypeStruct(n0.shape, F32), jax.ShapeDtypeStruct((B, H, LANES), F32)],
        scratch_shapes=[pltpu.VMEM((Lc + SUBLANES, QK), F32), pltpu.VMEM((H, dk, dv), F32),
                        pltpu.VMEM((H, dk), F32), pltpu.VMEM((H, LANES), F32)],
        compiler_params=_params(2), name="mlstm",
    )(qk, v, og, gt, cache8, conv_w, conv_b.reshape(1, QK), gate_bias, norm_g.reshape(1, H * dv), C0, n0, m0b)
    return y, C, n, m[:, :, 0]


def _norm_kernel(x_ref, g_ref, o_ref):
    o_ref[...] = _rms(x_ref[...], g_ref[...])


def _final_norm(x, g):
    M, D = x.shape
    tm = min(ROW_TILE, M)
    assert M % tm == 0
    return pl.pallas_call(
        _norm_kernel, grid=(M // tm,),
        in_specs=[pl.BlockSpec((tm, D), lambda i: (i, 0)), pl.BlockSpec((1, D), lambda i: (0, 0))],
        out_specs=pl.BlockSpec((tm, D), lambda i: (i, 0)),
        out_shape=jax.ShapeDtypeStruct((M, D), F32),
        compiler_params=_params(1), name="final_norm",
    )(x, g.reshape(1, D))


def _rotary_tables(pos, dk):
    half = dk // 2
    inv = 1.0 / (10000.0 ** jnp.linspace(0.0, 1.0, half, dtype=F32))
    ang = pos.astype(F32)[:, None] * inv[None, :]
    cos, sin = jnp.cos(ang), jnp.sin(ang)
    return jnp.concatenate([cos, cos], -1), jnp.concatenate([-sin, sin], -1)


def _trunk(x, pos0, ret_S, pool_buf, ml_C, ml_n, ml_m, ml_conv, ffn_buf, p):
    B, L, D = x.shape
    depth = p['norm_mix'].shape[0]
    _, _, RH, rdk, rdv = ret_S.shape
    _, _, MH, mdk, mdv = ml_C.shape
    PW = pool_buf.shape[-1]
    qw, rw, mqk, mv = RH * rdk, RH * rdv, MH * mdk, MH * mdv
    assert L >= pool_buf.shape[2] and L >= ml_conv.shape[2]
    cos, sin = _rotary_tables(pos0 + jnp.arange(L, dtype=jnp.int32), rdk)
    o_S, o_pool, o_C, o_n, o_m, o_conv, o_ffn = [], [], [], [], [], [], []
    for l in range(depth):
        x2 = x.reshape(B * L, D)
        if l % 2 == 0:
            a = l // 2
            w_in = p['w_in_a'][a]
            assert qw == rw == PW and w_in.shape[1] == 2 * qw + 2 * rw + PW
            ws = [w_in[:, s * qw:(s + 1) * qw] for s in range(5)]
            q, k, v, g, u = _proj(x2, p['norm_mix'][l], ws, [F32, F32, BF16, F32, F32])
            sh = lambda t: t.reshape(B, L, -1)
            ret, S = _retention(sh(q), sh(k), sh(v), sh(g), cos, sin, ret_S[a])
            u3 = sh(u)
            pool = _pool(u3, pool_buf[a], p['w_pool'][a], p['pool_scale'][a], pos0)
            x2 = _outproj([ret.reshape(B * L, rw), pool.reshape(B * L, PW)], p['w_out_a'][a], x2)
            o_S.append(S)
            o_pool.append(u3[:, L - pool_buf.shape[2]:])
        else:
            c = l // 2
            w_in = p['w_in_c'][c]
            assert mv == mqk * 2 and w_in.shape[1] == 2 * mqk + 2 * mv + 2 * MH
            ws = [w_in[:, s * mv:(s + 1) * mv] for s in range(3)]
            w_gates = jnp.pad(w_in[:, 3 * mv:], ((0, 0), (0, LANES - 2 * MH)))
            qk, v, og, gt = _proj(x2, p['norm_mix'][l], ws, [F32, BF16, F32], w_small=w_gates)
            gate_bias = jnp.pad(jnp.concatenate([p['b_i'][c], p['b_f'][c]]), (0, LANES - 2 * MH)).reshape(1, LANES)
            sh = lambda t: t.reshape(B, L, -1)
            qk3 = sh(qk)
            y, C, n, m = _mlstm(qk3, sh(v), sh(og), sh(gt), ml_conv[c], p['conv_c_w'][c], p['conv_c_b'][c],
                                gate_bias, p['ml_norm'][c], ml_C[c], ml_n[c], ml_m[c])
            x2 = _outproj([y.reshape(B * L, mv)], p['w_out_c'][c], x2)
            o_C.append(C)
            o_n.append(n)
            o_m.append(m)
            o_conv.append(qk3[:, L - ml_conv.shape[2]:])
        x, fb = _ffn(x2.reshape(B, L, D), p['norm_ffn'][l], p['w_up'][l], p['w_gate'][l], p['ffn_conv_w'][l],
                     p['ffn_conv_b'][l], p['w_down'][l], ffn_buf[l])
        o_ffn.append(fb)
    y = _final_norm(x.reshape(B * L, D), p['norm_final']).reshape(B, L, D)
    return (y, jnp.stack(o_S), jnp.stack(o_pool), jnp.stack(o_C), jnp.stack(o_n), jnp.stack(o_m),
            jnp.stack(o_conv), jnp.stack(o_ffn))


def kernel(x_prompt, x_sample, state_ret, cache_pool, state_mlstm_C, state_mlstm_n, state_mlstm_m, cache_mlstm_conv, cache_ffn_conv, norm_mix, norm_ffn, norm_final, w_in_a, w_pool, pool_scale, w_out_a, w_in_c, conv_c_w, conv_c_b, b_i, b_f, ml_norm, w_out_c, w_up, w_gate, ffn_conv_w, ffn_conv_b, w_down):
    bf = lambda w: w.astype(BF16)
    p = dict(norm_mix=norm_mix, norm_ffn=norm_ffn, norm_final=norm_final, w_in_a=bf(w_in_a), w_pool=bf(w_pool),
             pool_scale=pool_scale, w_out_a=bf(w_out_a), w_in_c=bf(w_in_c), conv_c_w=conv_c_w, conv_c_b=conv_c_b,
             b_i=b_i, b_f=b_f, ml_norm=ml_norm, w_out_c=bf(w_out_c), w_up=bf(w_up), w_gate=bf(w_gate),
             ffn_conv_w=ffn_conv_w, ffn_conv_b=ffn_conv_b, w_down=bf(w_down))
    B0 = x_prompt.shape[0]
    zeros = lambda a: jnp.zeros((a.shape[0], B0) + a.shape[2:], F32)
    prompt = _trunk(x_prompt, 0, zeros(state_ret), zeros(cache_pool), zeros(state_mlstm_C), zeros(state_mlstm_n),
                    zeros(state_mlstm_m), zeros(cache_mlstm_conv), zeros(cache_ffn_conv), p)
    sample = _trunk(x_sample, PAST_LEN, state_ret, cache_pool, state_mlstm_C, state_mlstm_n, state_mlstm_m,
                    cache_mlstm_conv, cache_ffn_conv, p)
    return (prompt[0], sample[0]) + tuple(prompt[1:]) + tuple(sample[1:])
```

```python
import functools
import math

import jax
import jax.numpy as jnp
from jax import lax
from jax.experimental import pallas as pl
from jax.experimental.pallas import tpu as pltpu

EPS = 1e-6
PAST_LEN = 2048
POOL_WINDOWS = (2, 4, 8, 16)
F32 = jnp.float32
BF16 = jnp.bfloat16

SUBLANES = 8
LANES = 128
POOL_HALO = 16
VMEM_LIMIT = 56 * 1024 * 1024

ROW_TILE = 512
PROJ_ROWS = 1024
PROJ_COLS = 256
OUT_COLS = 512
FFN_COLS = 512
FFN_CHUNK_ROWS = 16
MXU_COLS = 256
RET_CHUNK = 256
ML_CHUNK = 256
POOL_ROWS = 512


def _params(n_axes):
    return pltpu.CompilerParams(dimension_semantics=("arbitrary",) * n_axes, vmem_limit_bytes=VMEM_LIMIT)


def _rms(x, g):
    return x * lax.rsqrt(jnp.mean(x * x, -1, keepdims=True) + EPS) * g


def _dot(a, b):
    return jnp.dot(a, b, preferred_element_type=F32)


def _dot_nt(a, b):
    return lax.dot_general(a, b, (((1,), (1,)), ((), ())), preferred_element_type=F32)


def _dot_tn(a, b):
    return lax.dot_general(a, b, (((0,), (0,)), ((), ())), preferred_element_type=F32)


def _proj_kernel(*refs, n_seg, has_small):
    x_ref, g_ref = refs[0], refs[1]
    k = 2
    w_refs = refs[k:k + n_seg]
    k += n_seg
    ws_ref = refs[k] if has_small else None
    k += has_small
    o_refs = refs[k:k + n_seg]
    k += n_seg
    os_ref = refs[k] if has_small else None
    k += has_small
    h_ref = refs[k]

    @pl.when(pl.program_id(1) == 0)
    def _():
        h_ref[...] = _rms(x_ref[...], g_ref[...]).astype(BF16)
        if has_small:
            os_ref[...] = _dot(h_ref[...], ws_ref[...])

    h = h_ref[...]
    for w_ref, o_ref in zip(w_refs, o_refs):
        o_ref[...] = _dot(h, w_ref[...]).astype(o_ref.dtype)


def _proj(x, g, w, wseg, out_dtypes, w_small=None):
    M, D = x.shape
    n_seg = len(out_dtypes)
    tm, tn = min(PROJ_ROWS, M), min(PROJ_COLS, wseg)
    assert M % tm == 0 and wseg % tn == 0 and w.shape[1] >= n_seg * wseg
    has_small = w_small is not None
    nt = wseg // tn
    in_specs = [pl.BlockSpec((tm, D), lambda i, j: (i, 0)), pl.BlockSpec((1, D), lambda i, j: (0, 0))]
    in_specs += [pl.BlockSpec((D, tn), functools.partial(lambda i, j, s: (0, s * nt + j), s=s)) for s in range(n_seg)]
    out_specs = [pl.BlockSpec((tm, tn), lambda i, j: (i, j)) for _ in range(n_seg)]
    out_shape = [jax.ShapeDtypeStruct((M, wseg), dt) for dt in out_dtypes]
    args = [x, g.reshape(1, D)] + [w] * n_seg
    if has_small:
        ns = w_small.shape[1]
        in_specs.append(pl.BlockSpec((D, ns), lambda i, j: (0, 0)))
        out_specs.append(pl.BlockSpec((tm, ns), lambda i, j: (i, 0)))
        out_shape.append(jax.ShapeDtypeStruct((M, ns), F32))
        args.append(w_small)
    return pl.pallas_call(
        functools.partial(_proj_kernel, n_seg=n_seg, has_small=has_small),
        grid=(M // tm, nt),
        in_specs=in_specs, out_specs=out_specs, out_shape=out_shape,
        scratch_shapes=[pltpu.VMEM((tm, D), BF16)],
        compiler_params=_params(2), name="norm_proj",
    )(*args)


def _outproj_kernel(*refs, n_in, tn):
    a_refs = refs[:n_in]
    w_ref, x_ref, o_ref = refs[n_in], refs[n_in + 1], refs[n_in + 2]
    for n0 in range(0, o_ref.shape[1], tn):
        acc = x_ref[:, n0:n0 + tn]
        off = 0
        for a_ref in a_refs:
            kk = a_ref.shape[1]
            acc = acc + _dot(a_ref[...], w_ref[off:off + kk, n0:n0 + tn])
            off += kk
        o_ref[:, n0:n0 + tn] = acc


def _outproj(acts, w, x):
    M, N = x.shape
    K = w.shape[0]
    tm, tn = min(ROW_TILE, M), min(OUT_COLS, N)
    assert M % tm == 0 and N % tn == 0 and sum(a.shape[1] for a in acts) == K
    in_specs = [pl.BlockSpec((tm, a.shape[1]), lambda i: (i, 0)) for a in acts]
    in_specs += [pl.BlockSpec((K, N), lambda i: (0, 0)), pl.BlockSpec((tm, N), lambda i: (i, 0))]
    return pl.pallas_call(
        functools.partial(_outproj_kernel, n_in=len(acts), tn=tn),
        grid=(M // tm,),
        in_specs=in_specs, out_specs=pl.BlockSpec((tm, N), lambda i: (i, 0)),
        out_shape=jax.ShapeDtypeStruct((M, N), F32),
        compiler_params=_params(1), name="out_proj",
    )(*acts, w, x)


def _ffn_kernel(x_ref, g_ref, wu_ref, wg_ref, cw_ref, cb_ref, wd_ref, cache_ref, o_ref, nc_ref,
                h_ref, ext0_ref, ext1_ref, up0_ref, up1_ref, act0_ref, act1_ref, carry_ref, *, S, R, W, nj):
    r, j = pl.program_id(1), pl.program_id(2)
    D = x_ref.shape[-1]
    tf = wu_ref.shape[1]
    H8 = SUBLANES
    raw = ((ext0_ref, up0_ref), (ext1_ref, up1_ref))
    act = (act0_ref, act1_ref)

    def up_gate_pieces(ext_ref, up_ref):
        def up_piece(c0):
            up_ref[:, c0:c0 + MXU_COLS] = _dot(h_ref[...], wu_ref[:, c0:c0 + MXU_COLS])

        def gate_piece(c0):
            ext_ref[:, H8:H8 + R, c0:c0 + MXU_COLS] = _dot(h_ref[...], wg_ref[:, c0:c0 + MXU_COLS]).reshape(S, R, MXU_COLS)

        cols = range(0, tf, MXU_COLS)
        return [functools.partial(up_piece, c0) for c0 in cols] + [functools.partial(gate_piece, c0) for c0 in cols]

    def conv_act_pieces(ext_ref, up_ref, act_ref, jj):
        def halo():
            ext_ref[:, 0:H8, :] = carry_ref[jj]
            carry_ref[jj] = ext_ref[:, R:R + H8, :]
            nc_ref[:, 0] = ext_ref[:, H8 + R - (W - 1):H8 + R, :]

        def chunk(c):
            s, r0 = divmod(c * FFN_CHUNK_ROWS, R)
            acc = None
            for t in range(W):
                term = ext_ref[s, pl.ds(r0 + H8 - (W - 1) + t, FFN_CHUNK_ROWS), :] * cw_ref[t:t + 1, :]
                acc = term if acc is None else acc + term
            rows = slice(s * R + r0, s * R + r0 + FFN_CHUNK_ROWS)
            act_ref[rows, :] = (jax.nn.gelu(acc + cb_ref[...]) * up_ref[rows, :]).astype(BF16)

        def first():
            halo()
            chunk(0)

        return [first] + [functools.partial(chunk, c) for c in range(1, S * R // FFN_CHUNK_ROWS)]

    def down_pieces(act_ref):
        def piece(c0):
            o_ref[:, :, c0:c0 + MXU_COLS] += _dot(act_ref[...], wd_ref[:, c0:c0 + MXU_COLS]).reshape(S, R, MXU_COLS)

        return [functools.partial(piece, c0) for c0 in range(0, D, MXU_COLS)]

    def stages(p, has_a, has_b, has_c):
        mxu = (up_gate_pieces(*raw[p]) if has_a else []) + (down_pieces(act[p]) if has_c else [])
        vpu = conv_act_pieces(*raw[1 - p], act[1 - p], j - 1) if has_b else []
        if not mxu:
            order = vpu
        else:
            order = []
            for i, piece in enumerate(mxu):
                order.append(piece)
                order += [v for k, v in enumerate(vpu) if (k * len(mxu)) // len(vpu) == i]
        for piece in order:
            piece()

    @pl.when(j == 0)
    def _():
        @pl.when(r == 0)
        def _():
            for jj in range(nj):
                carry_ref[jj] = cache_ref[:, :, jj * tf:(jj + 1) * tf]

        x = x_ref[...].reshape(S * R, D)
        h_ref[...] = _rms(x, g_ref[...]).astype(BF16)
        o_ref[...] = x_ref[...]
        stages(0, True, False, False)

    @pl.when(j == 1)
    def _():
        stages(1, True, True, False)

    for parity in (0, 1):
        @pl.when(jnp.logical_and(jnp.logical_and(j >= 2, j < nj), j % 2 == parity))
        def _():
            stages(parity, True, True, True)

    @pl.when(j == nj)
    def _():
        stages(nj % 2, False, True, True)

    @pl.when(j == nj + 1)
    def _():
        stages((nj + 1) % 2, False, False, True)


def _ffn(x, g, w_up, w_gate, conv_w, conv_b, w_down, cache):
    B, L, D = x.shape
    F = w_up.shape[1]
    W = conv_w.shape[0]
    R = min(ROW_TILE, L)
    S = min(B, max(1, ROW_TILE // R))
    tf = min(FFN_COLS, F)
    assert L % R == 0 and B % S == 0 and F % tf == 0 and R % SUBLANES == 0 and R >= SUBLANES >= W - 1
    nj = F // tf
    assert nj >= 2 and tf % MXU_COLS == 0 and D % MXU_COLS == 0
    assert R % FFN_CHUNK_ROWS == 0
    cache8 = jnp.pad(cache, ((0, 0), (SUBLANES - (W - 1), 0), (0, 0)))
    tile = lambda j, lag: jnp.clip(j - lag, 0, nj - 1)
    y, tails = pl.pallas_call(
        functools.partial(_ffn_kernel, S=S, R=R, W=W, nj=nj),
        grid=(B // S, L // R, nj + 2),
        in_specs=[
            pl.BlockSpec((S, R, D), lambda b, r, j: (b, r, 0)),
            pl.BlockSpec((1, D), lambda b, r, j: (0, 0)),
            pl.BlockSpec((D, tf), lambda b, r, j: (0, tile(j, 0))),
            pl.BlockSpec((D, tf), lambda b, r, j: (0, tile(j, 0))),
            pl.BlockSpec((W, tf), lambda b, r, j: (0, tile(j, 1))),
            pl.BlockSpec((1, tf), lambda b, r, j: (0, tile(j, 1))),
            pl.BlockSpec((tf, D), lambda b, r, j: (tile(j, 2), 0)),
            pl.BlockSpec((S, SUBLANES, F), lambda b, r, j: (b, 0, 0)),
        ],
        out_specs=[
            pl.BlockSpec((S, R, D), lambda b, r, j: (b, r, 0)),
            pl.BlockSpec((S, 1, W - 1, tf), lambda b, r, j: (b, r, 0, tile(j, 1))),
        ],
        out_shape=[jax.ShapeDtypeStruct((B, L, D), F32), jax.ShapeDtypeStruct((B, L // R, W - 1, F), F32)],
        scratch_shapes=[
            pltpu.VMEM((S * R, D), BF16),
            pltpu.VMEM((S, R + SUBLANES, tf), F32),
            pltpu.VMEM((S, R + SUBLANES, tf), F32),
            pltpu.VMEM((S * R, tf), F32),
            pltpu.VMEM((S * R, tf), F32),
            pltpu.VMEM((S * R, tf), BF16),
            pltpu.VMEM((S * R, tf), BF16),
            pltpu.VMEM((nj, S, SUBLANES, tf), F32),
        ],
        compiler_params=_params(3), name="conv_ffn",
    )(x, g.reshape(1, D), w_up, w_gate, conv_w, conv_b.reshape(1, F), w_down, cache8)
    return y, tails[:, -1]


def _ret_kernel(q_ref, k_ref, v_ref, g_ref, cos_ref, sin_ref, s0_ref, o_ref, so_ref, S_ref, *, H, dk, dv, Lc):
    c = pl.program_id(1)

    @pl.when(c == 0)
    def _():
        S_ref[...] = s0_ref[0]

    idx = lax.broadcasted_iota(jnp.int32, (Lc, 1), 0).astype(F32)
    diff = (lax.broadcasted_iota(jnp.int32, (Lc, Lc), 0) - lax.broadcasted_iota(jnp.int32, (Lc, Lc), 1)).astype(F32)
    cos, sin = cos_ref[...], sin_ref[...]
    for h in range(H):
        lg = math.log1p(-(2.0 ** (-5.0 - h)))
        q = q_ref[0, :, h * dk:(h + 1) * dk].astype(F32)
        k = k_ref[0, :, h * dk:(h + 1) * dk].astype(F32)
        qr = q * cos + pltpu.roll(q, dk // 2, 1) * sin
        kr = (k * cos + pltpu.roll(k, dk // 2, 1) * sin) * (dk ** -0.5)
        v = v_ref[0, :, h * dv:(h + 1) * dv]
        dec = jnp.where(diff >= 0, jnp.exp(lg * jnp.maximum(diff, 0.0)), 0.0)
        qb = qr.astype(BF16)
        s = _dot_nt(qb, kr.astype(BF16)) * dec
        Sh = S_ref[h]
        o = _dot(s.astype(BF16), v) + _dot(qb, Sh.astype(BF16)) * jnp.exp((idx + 1.0) * lg)
        kd = kr * jnp.exp((Lc - 1.0 - idx) * lg)
        S_ref[h] = math.exp(Lc * lg) * Sh + _dot_tn(kd.astype(BF16), v)
        on = o * lax.rsqrt(jnp.mean(o * o, -1, keepdims=True) + EPS)
        gg = g_ref[0, :, h * dv:(h + 1) * dv].astype(F32)
        o_ref[0, :, h * dv:(h + 1) * dv] = (on * (gg * jax.nn.sigmoid(gg))).astype(o_ref.dtype)

    @pl.when(c == pl.num_programs(1) - 1)
    def _():
        so_ref[0] = S_ref[...]


def _retention(q, k, v, g, cos, sin, S0):
    B, L, _ = q.shape
    _, H, dk, dv = S0.shape
    Lc = min(RET_CHUNK, L)
    assert L % Lc == 0
    seq = lambda w: pl.BlockSpec((1, Lc, w), lambda b, c: (b, c, 0))
    tab = pl.BlockSpec((Lc, dk), lambda b, c: (c, 0))
    st = pl.BlockSpec((1, H, dk, dv), lambda b, c: (b, 0, 0, 0))
    return pl.pallas_call(
        functools.partial(_ret_kernel, H=H, dk=dk, dv=dv, Lc=Lc),
        grid=(B, L // Lc),
        in_specs=[seq(H * dk), seq(H * dk), seq(H * dv), seq(H * dv), tab, tab, st],
        out_specs=[seq(H * dv), st],
        out_shape=[jax.ShapeDtypeStruct((B, L, H * dv), BF16), jax.ShapeDtypeStruct(S0.shape, F32)],
        scratch_shapes=[pltpu.VMEM((H, dk, dv), F32)],
        compiler_params=_params(2), name="retention",
    )(q, k, v, g, cos, sin, S0)


def _pool_kernel(u_ref, cache_ref, wp_ref, ps_ref, o_ref, ext_ref, *, tr, gd, pos0):
    r = pl.program_id(1)
    PB = POOL_HALO

    @pl.when(r == 0)
    def _():
        ext_ref[0:PB, :] = cache_ref[0]

    @pl.when(r > 0)
    def _():
        ext_ref[0:PB, :] = ext_ref[tr:tr + PB, :]

    ext_ref[PB:PB + tr, :] = u_ref[0]
    pos = pos0 + r * tr + lax.broadcasted_iota(jnp.int32, (tr, 1), 0)
    for j, w in enumerate(POOL_WINDOWS):
        lo, hi = j * gd, (j + 1) * gd
        acc = ext_ref[PB:PB + tr, lo:hi]
        for d in range(1, w):
            acc = acc + ext_ref[pl.ds(PB - d, tr), lo:hi]
        cnt = jnp.minimum(pos + 1, w).astype(F32)
        pooled = acc / cnt - u_ref[0, :, lo:hi]
        y = _dot(pooled.astype(BF16), wp_ref[j]) * ps_ref[:, lo:hi]
        o_ref[0, :, lo:hi] = y.astype(o_ref.dtype)


def _pool(u, cache, w_pool, pool_scale, pos0):
    B, L, P = u.shape
    G, gd, _ = w_pool.shape
    assert G == len(POOL_WINDOWS) and G * gd == P and cache.shape[1] == max(POOL_WINDOWS) - 1
    tr = min(POOL_ROWS, L)
    assert L % tr == 0 and tr >= POOL_HALO
    cache_h = jnp.pad(cache, ((0, 0), (POOL_HALO - cache.shape[1], 0), (0, 0)))
    return pl.pallas_call(
        functools.partial(_pool_kernel, tr=tr, gd=gd, pos0=pos0),
        grid=(B, L // tr),
        in_specs=[
            pl.BlockSpec((1, tr, P), lambda b, r: (b, r, 0)),
            pl.BlockSpec((1, POOL_HALO, P), lambda b, r: (b, 0, 0)),
            pl.BlockSpec((G, gd, gd), lambda b, r: (0, 0, 0)),
            pl.BlockSpec((1, P), lambda b, r: (0, 0)),
        ],
        out_specs=pl.BlockSpec((1, tr, P), lambda b, r: (b, r, 0)),
        out_shape=jax.ShapeDtypeStruct((B, L, P), BF16),
        scratch_shapes=[pltpu.VMEM((tr + POOL_HALO, P), F32)],
        compiler_params=_params(2), name="ms_pool",
    )(u, cache_h, w_pool, pool_scale.reshape(1, P))


def _mlstm_kernel(qk_ref, v_ref, og_ref, gt_ref, cache_ref, cw_ref, cb_ref, bias_ref, ng_ref,
                  C0_ref, n0_ref, m0_ref, y_ref, Co_ref, no_ref, mo_ref,
                  ext_ref, C_ref, n_ref, m_ref, *, H, dk, dv, Lc, W):
    c = pl.program_id(1)
    H8 = SUBLANES

    @pl.when(c == 0)
    def _():
        ext_ref[0:H8, :] = cache_ref[0]
        C_ref[...] = C0_ref[0]
        n_ref[...] = n0_ref[0]
        m_ref[...] = m0_ref[0]

    @pl.when(c > 0)
    def _():
        ext_ref[0:H8, :] = ext_ref[Lc:Lc + H8, :]

    ext_ref[H8:H8 + Lc, :] = qk_ref[0]

    def conv_silu(lo, hi):
        acc = None
        for t in range(W):
            term = ext_ref[pl.ds(H8 - (W - 1) + t, Lc), lo:hi] * cw_ref[t:t + 1, lo:hi]
            acc = term if acc is None else acc + term
        pre = acc + cb_ref[:, lo:hi]
        return pre * jax.nn.sigmoid(pre)

    gt = gt_ref[0] + bias_ref[...]
    lane = lax.broadcasted_iota(jnp.int32, gt.shape, 1)
    row = lax.broadcasted_iota(jnp.int32, gt.shape, 0)
    b = jax.nn.log_sigmoid(gt)
    sh = 1
    while sh < Lc:
        b = b + jnp.where(row >= sh, pltpu.roll(b, sh, 0), 0.0)
        sh *= 2
    A = jnp.where(lane < H, gt, b)
    AT = A.T
    causal = lax.broadcasted_iota(jnp.int32, (Lc, Lc), 0) >= lax.broadcasted_iota(jnp.int32, (Lc, Lc), 1)

    for h in range(H):
        i_col, b_col = A[:, h:h + 1], A[:, H + h:H + h + 1]
        i_row, b_row = AT[h:h + 1, :], AT[H + h:H + h + 1, :]
        m_prev = m_ref[h:h + 1, 0:1]
        dlog = jnp.where(causal, b_col - b_row + i_row, -jnp.inf)
        inter_log = b_col + m_prev
        m_t = jnp.maximum(inter_log, jnp.max(dlog, -1, keepdims=True))
        dw = jnp.exp(dlog - m_t)
        iw = jnp.exp(inter_log - m_t)
        q = conv_silu(h * dk, (h + 1) * dk)
        k = conv_silu((H + h) * dk, (H + h + 1) * dk) * (dk ** -0.5)
        qb = q.astype(BF16)
        s = _dot_nt(qb, k.astype(BF16)) * dw
        v = v_ref[0, :, h * dv:(h + 1) * dv]
        Ch = C_ref[h]
        nh = n_ref[h:h + 1, :]
        num = _dot(s.astype(BF16), v) + _dot(qb, Ch.astype(BF16)) * iw
        den = jnp.sum(s, -1, keepdims=True) + jnp.sum(q * nh, -1, keepdims=True) * iw
        hh = num / jnp.maximum(jnp.abs(den), jnp.exp(-m_t))
        m_L, b_L = m_t[Lc - 1:Lc, :], b_col[Lc - 1:Lc, :]
        kw = k * jnp.exp(b_L - b_col + i_col - m_L)
        decay = jnp.exp(b_L + m_prev - m_L)
        C_ref[h] = decay * Ch + _dot_tn(kw.astype(BF16), v)
        n_ref[h:h + 1, :] = decay * nh + jnp.sum(kw, 0, keepdims=True)
        m_ref[h:h + 1, :] = jnp.broadcast_to(m_L, (1, m_ref.shape[1]))
        hn = hh * lax.rsqrt(jnp.mean(hh * hh, -1, keepdims=True) + EPS) * ng_ref[:, h * dv:(h + 1) * dv]
        og = og_ref[0, :, h * dv:(h + 1) * dv].astype(F32)
        y_ref[0, :, h * dv:(h + 1) * dv] = (hn * jax.nn.sigmoid(og)).astype(y_ref.dtype)

    @pl.when(c == pl.num_programs(1) - 1)
    def _():
        Co_ref[0] = C_ref[...]
        no_ref[0] = n_ref[...]
        mo_ref[0] = m_ref[...]


def _mlstm(qk, v, og, gt, cache, conv_w, conv_b, gate_bias, norm_g, C0, n0, m0):
    B, L, QK = qk.shape
    _, H, dk, dv = C0.shape
    W = conv_w.shape[0]
    NG = gt.shape[-1]
    Lc = min(ML_CHUNK, L)
    assert L % Lc == 0 and Lc >= SUBLANES >= W - 1 and QK == 2 * H * dk and H == SUBLANES and dk == LANES
    cache8 = jnp.pad(cache, ((0, 0), (SUBLANES - (W - 1), 0), (0, 0)))
    m0b = jnp.broadcast_to(m0[:, :, None], (B, H, LANES))
    seq = lambda w: pl.BlockSpec((1, Lc, w), lambda b, c: (b, c, 0))
    const = lambda shape: pl.BlockSpec(shape, lambda b, c: (0,) * len(shape))
    per_b = lambda shape: pl.BlockSpec((1,) + shape, lambda b, c: (b,) + (0,) * len(shape))
    y, C, n, m = pl.pallas_call(
        functools.partial(_mlstm_kernel, H=H, dk=dk, dv=dv, Lc=Lc, W=W),
        grid=(B, L // Lc),
        in_specs=[seq(QK), seq(H * dv), seq(H * dv), seq(NG), per_b((SUBLANES, QK)),
                  const((W, QK)), const((1, QK)), const((1, NG)), const((1, H * dv)),
                  per_b((H, dk, dv)), per_b((H, dk)), per_b((H, LANES))],
        out_specs=[seq(H * dv), per_b((H, dk, dv)), per_b((H, dk)), per_b((H, LANES))],
        out_shape=[jax.ShapeDtypeStruct((B, L, H * dv), BF16), jax.ShapeDtypeStruct(C0.shape, F32),
                   jax.ShapeDtypeStruct(n0.shape, F32), jax.ShapeDtypeStruct((B, H, LANES), F32)],
        scratch_shapes=[pltpu.VMEM((Lc + SUBLANES, QK), F32), pltpu.VMEM((H, dk, dv), F32),
                        pltpu.VMEM((H, dk), F32), pltpu.VMEM((H, LANES), F32)],
        compiler_params=_params(2), name="mlstm",
    )(qk, v, og, gt, cache8, conv_w, conv_b.reshape(1, QK), gate_bias, norm_g.reshape(1, H * dv), C0, n0, m0b)
    return y, C, n, m[:, :, 0]


def _norm_kernel(x_ref, g_ref, o_ref):
    o_ref[...] = _rms(x_ref[...], g_ref[...])


def _final_norm(x, g):
    M, D = x.shape
    tm = min(ROW_TILE, M)
    assert M % tm == 0
    return pl.pallas_call(
        _norm_kernel, grid=(M // tm,),
        in_specs=[pl.BlockSpec((tm, D), lambda i: (i, 0)), pl.BlockSpec((1, D), lambda i: (0, 0))],
        out_specs=pl.BlockSpec((tm, D), lambda i: (i, 0)),
        out_shape=jax.ShapeDtypeStruct((M, D), F32),
        compiler_params=_params(1), name="final_norm",
    )(x, g.reshape(1, D))


def _rotary_tables(pos, dk):
    half = dk // 2
    inv = 1.0 / (10000.0 ** jnp.linspace(0.0, 1.0, half, dtype=F32))
    ang = pos.astype(F32)[:, None] * inv[None, :]
    cos, sin = jnp.cos(ang), jnp.sin(ang)
    return jnp.concatenate([cos, cos], -1), jnp.concatenate([-sin, sin], -1)


def _trunk(x, pos0, ret_S, pool_buf, ml_C, ml_n, ml_m, ml_conv, ffn_buf, p):
    B, L, D = x.shape
    depth = p['norm_mix'].shape[0]
    _, _, RH, rdk, rdv = ret_S.shape
    _, _, MH, mdk, mdv = ml_C.shape
    PW = pool_buf.shape[-1]
    qw, rw, mqk, mv = RH * rdk, RH * rdv, MH * mdk, MH * mdv
    assert L >= pool_buf.shape[2] and L >= ml_conv.shape[2]
    cos, sin = _rotary_tables(pos0 + jnp.arange(L, dtype=jnp.int32), rdk)
    o_S, o_pool, o_C, o_n, o_m, o_conv, o_ffn = [], [], [], [], [], [], []
    for l in range(depth):
        x2 = x.reshape(B * L, D)
        if l % 2 == 0:
            a = l // 2
            w_in = p['w_in_a'][a]
            assert qw == rw == PW and w_in.shape[1] == 2 * qw + 2 * rw + PW
            q, k, v, g, u = _proj(x2, p['norm_mix'][l], w_in, qw, [BF16, BF16, BF16, BF16, F32])
            sh = lambda t: t.reshape(B, L, -1)
            ret, S = _retention(sh(q), sh(k), sh(v), sh(g), cos, sin, ret_S[a])
            u3 = sh(u)
            pool = _pool(u3, pool_buf[a], p['w_pool'][a], p['pool_scale'][a], pos0)
            x2 = _outproj([ret.reshape(B * L, rw), pool.reshape(B * L, PW)], p['w_out_a'][a], x2)
            o_S.append(S)
            o_pool.append(u3[:, L - pool_buf.shape[2]:])
        else:
            c = l // 2
            w_in = p['w_in_c'][c]
            assert mv == mqk * 2 and w_in.shape[1] == 2 * mqk + 2 * mv + 2 * MH
            w_gates = jnp.pad(w_in[:, 3 * mv:], ((0, 0), (0, LANES - 2 * MH)))
            qk, v, og, gt = _proj(x2, p['norm_mix'][l], w_in, mv, [F32, BF16, BF16], w_small=w_gates)
            gate_bias = jnp.pad(jnp.concatenate([p['b_i'][c], p['b_f'][c]]), (0, LANES - 2 * MH)).reshape(1, LANES)
            sh = lambda t: t.reshape(B, L, -1)
            qk3 = sh(qk)
            y, C, n, m = _mlstm(qk3, sh(v), sh(og), sh(gt), ml_conv[c], p['conv_c_w'][c], p['conv_c_b'][c],
                                gate_bias, p['ml_norm'][c], ml_C[c], ml_n[c], ml_m[c])
            x2 = _outproj([y.reshape(B * L, mv)], p['w_out_c'][c], x2)
            o_C.append(C)
            o_n.append(n)
            o_m.append(m)
            o_conv.append(qk3[:, L - ml_conv.shape[2]:])
        x, fb = _ffn(x2.reshape(B, L, D), p['norm_ffn'][l], p['w_up'][l], p['w_gate'][l], p['ffn_conv_w'][l],
                     p['ffn_conv_b'][l], p['w_down'][l], ffn_buf[l])
        o_ffn.append(fb)
    y = _final_norm(x.reshape(B * L, D), p['norm_final']).reshape(B, L, D)
    return (y, jnp.stack(o_S), jnp.stack(o_pool), jnp.stack(o_C), jnp.stack(o_n), jnp.stack(o_m),
            jnp.stack(o_conv), jnp.stack(o_ffn))


def kernel(x_prompt, x_sample, state_ret, cache_pool, state_mlstm_C, state_mlstm_n, state_mlstm_m, cache_mlstm_conv, cache_ffn_conv, norm_mix, norm_ffn, norm_final, w_in_a, w_pool, pool_scale, w_out_a, w_in_c, conv_c_w, conv_c_b, b_i, b_f, ml_norm, w_out_c, w_up, w_gate, ffn_conv_w, ffn_conv_b, w_down):
    bf = lambda w: w.astype(BF16)
    p = dict(norm_mix=norm_mix, norm_ffn=norm_ffn, norm_final=norm_final, w_in_a=bf(w_in_a), w_pool=bf(w_pool),
             pool_scale=pool_scale, w_out_a=bf(w_out_a), w_in_c=bf(w_in_c), conv_c_w=conv_c_w, conv_c_b=conv_c_b,
             b_i=b_i, b_f=b_f, ml_norm=ml_norm, w_out_c=bf(w_out_c), w_up=bf(w_up), w_gate=bf(w_gate),
             ffn_conv_w=ffn_conv_w, ffn_conv_b=ffn_conv_b, w_down=bf(w_down))
    B0 = x_prompt.shape[0]
    zeros = lambda a: jnp.zeros((a.shape[0], B0) + a.shape[2:], F32)
    prompt = _trunk(x_prompt, 0, zeros(state_ret), zeros(cache_pool), zeros(state_mlstm_C), zeros(state_mlstm_n),
                    zeros(state_mlstm_m), zeros(cache_mlstm_conv), zeros(cache_ffn_conv), p)
    sample = _trunk(x_sample, PAST_LEN, state_ret, cache_pool, state_mlstm_C, state_mlstm_n, state_mlstm_m,
                    cache_mlstm_conv, cache_ffn_conv, p)
    return (prompt[0], sample[0]) + tuple(prompt[1:]) + tuple(sample[1:])
```

```python
import functools
import math

import jax
import jax.numpy as jnp
from jax import lax
from jax.experimental import pallas as pl
from jax.experimental.pallas import tpu as pltpu

EPS = 1e-6
PAST_LEN = 2048
POOL_WINDOWS = (2, 4, 8, 16)
F32 = jnp.float32
BF16 = jnp.bfloat16

SUBLANES = 8
LANES = 128
POOL_HALO = 16
VMEM_LIMIT = 56 * 1024 * 1024

ROW_TILE = 512
PROJ_ROWS = 1024
PROJ_COLS = 256
OUT_COLS = 512
FFN_COLS = 512
FFN_CHUNK_VREGS = 16
RET_CHUNK = 256
ML_CHUNK = 256
POOL_ROWS = 512


def _params(n_axes):
    return pltpu.CompilerParams(dimension_semantics=("arbitrary",) * n_axes, vmem_limit_bytes=VMEM_LIMIT)


def _rms(x, g):
    return x * lax.rsqrt(jnp.mean(x * x, -1, keepdims=True) + EPS) * g


def _dot(a, b):
    return jnp.dot(a, b, preferred_element_type=F32)


def _dot_nt(a, b):
    return lax.dot_general(a, b, (((1,), (1,)), ((), ())), preferred_element_type=F32)


def _dot_tn(a, b):
    return lax.dot_general(a, b, (((0,), (0,)), ((), ())), preferred_element_type=F32)


def _proj_kernel(*refs, n_seg, has_small):
    x_ref, g_ref = refs[0], refs[1]
    k = 2
    w_refs = refs[k:k + n_seg]
    k += n_seg
    ws_ref = refs[k] if has_small else None
    k += has_small
    o_refs = refs[k:k + n_seg]
    k += n_seg
    os_ref = refs[k] if has_small else None
    k += has_small
    h_ref = refs[k]

    @pl.when(pl.program_id(1) == 0)
    def _():
        h_ref[...] = _rms(x_ref[...], g_ref[...]).astype(BF16)
        if has_small:
            os_ref[...] = _dot(h_ref[...], ws_ref[...])

    h = h_ref[...]
    for w_ref, o_ref in zip(w_refs, o_refs):
        o_ref[...] = _dot(h, w_ref[...]).astype(o_ref.dtype)


def _proj(x, g, w, layer, wseg, out_dtypes, small_cols=0):
    M, D = x.shape
    n_seg = len(out_dtypes)
    tm, tn = min(PROJ_ROWS, M), min(PROJ_COLS, wseg)
    assert M % tm == 0 and wseg % tn == 0 and w.shape[2] >= n_seg * wseg + small_cols
    has_small = small_cols > 0
    nt = wseg // tn
    in_specs = [pl.BlockSpec((tm, D), lambda i, j: (i, 0)), pl.BlockSpec((1, D), lambda i, j: (0, 0))]
    in_specs += [pl.BlockSpec((None, D, tn), functools.partial(lambda i, j, s: (layer, 0, s * nt + j), s=s))
                 for s in range(n_seg)]
    out_specs = [pl.BlockSpec((tm, tn), lambda i, j: (i, j)) for _ in range(n_seg)]
    out_shape = [jax.ShapeDtypeStruct((M, wseg), dt) for dt in out_dtypes]
    args = [x, g.reshape(1, D)] + [w] * n_seg
    if has_small:
        assert (n_seg * wseg) % small_cols == 0
        in_specs.append(pl.BlockSpec((None, D, small_cols), lambda i, j: (layer, 0, n_seg * wseg // small_cols)))
        out_specs.append(pl.BlockSpec((tm, small_cols), lambda i, j: (i, 0)))
        out_shape.append(jax.ShapeDtypeStruct((M, small_cols), F32))
        args.append(w)
    return pl.pallas_call(
        functools.partial(_proj_kernel, n_seg=n_seg, has_small=has_small),
        grid=(M // tm, nt),
        in_specs=in_specs, out_specs=out_specs, out_shape=out_shape,
        scratch_shapes=[pltpu.VMEM((tm, D), BF16)],
        compiler_params=_params(2), name="norm_proj",
    )(*args)


def _ffn_kernel(*refs, n_act, S, R, W, nj, cr, tn):
    a_refs = refs[:n_act]
    (wo_ref, x_ref, g_ref, wu_ref, wg_ref, cw_ref, cb_ref, wd_ref, cache_ref, o_ref, nc_ref,
     h_ref, ext_ref, up_ref, act_ref, carry_ref) = refs[n_act:]
    r, j = pl.program_id(1), pl.program_id(2)
    D = x_ref.shape[-1]
    tf = wu_ref.shape[1]
    H8 = SUBLANES

    @pl.when(j == 0)
    def _():
        @pl.when(r == 0)
        def _():
            for jj in range(nj):
                carry_ref[jj] = cache_ref[:, :, jj * tf:(jj + 1) * tf]

        for n0 in range(0, D, tn):
            acc = x_ref[:, :, n0:n0 + tn].reshape(S * R, tn)
            off = 0
            for a_ref in a_refs:
                kk = a_ref.shape[1]
                acc = acc + _dot(a_ref[...], wo_ref[off:off + kk, n0:n0 + tn])
                off += kk
            o_ref[:, :, n0:n0 + tn] = acc.reshape(S, R, tn)
        h_ref[...] = _rms(o_ref[...].reshape(S * R, D), g_ref[...]).astype(BF16)

    h = h_ref[...]
    up_ref[...] = _dot(h, wu_ref[...])
    ext_ref[:, H8:H8 + R, :] = _dot(h, wg_ref[...]).reshape(S, R, tf)
    ext_ref[:, 0:H8, :] = carry_ref[j]
    carry_ref[j] = ext_ref[:, R:R + H8, :]
    nc_ref[:, 0] = ext_ref[:, H8 + R - (W - 1):H8 + R, :]
    for c in range(S * R // cr):
        s, r0 = divmod(c * cr, R)
        acc = None
        for t in range(W):
            term = ext_ref[s, pl.ds(r0 + H8 - (W - 1) + t, cr), :] * cw_ref[t:t + 1, :]
            acc = term if acc is None else acc + term
        rows = slice(s * R + r0, s * R + r0 + cr)
        act_ref[rows, :] = (jax.nn.gelu(acc + cb_ref[...]) * up_ref[rows, :]).astype(BF16)
    o_ref[...] += _dot(act_ref[...], wd_ref[...]).reshape(S, R, D)


def _ffn(acts, w_out, lo, x, g, w_up, w_gate, conv_w, conv_b, w_down, lf, cache):
    B, L, D = x.shape
    F = w_up.shape[2]
    W = conv_w.shape[1]
    K = w_out.shape[1]
    R = min(ROW_TILE, L)
    S = min(B, max(1, ROW_TILE // R))
    tf, tn = min(FFN_COLS, F), min(OUT_COLS, D)
    assert L % R == 0 and B % S == 0 and F % tf == 0 and D % tn == 0 and R % SUBLANES == 0 and R >= SUBLANES >= W - 1
    assert sum(a.shape[1] for a in acts) == K and (S == 1 or R == L)
    nj = F // tf
    cr = min(R, FFN_CHUNK_VREGS * SUBLANES * LANES // tf)
    assert R % cr == 0 and cr % (2 * SUBLANES) == 0
    cache8 = jnp.pad(cache, ((0, 0), (SUBLANES - (W - 1), 0), (0, 0)))
    nr = L // R
    in_specs = [pl.BlockSpec((S * R, a.shape[1]), lambda b, r, j: (b * nr + r, 0)) for a in acts]
    in_specs += [
        pl.BlockSpec((None, K, D), lambda b, r, j: (lo, 0, 0), pipeline_mode=pl.Buffered(1)),
        pl.BlockSpec((S, R, D), lambda b, r, j: (b, r, 0)),
        pl.BlockSpec((1, D), lambda b, r, j: (0, 0)),
        pl.BlockSpec((None, D, tf), lambda b, r, j: (lf, 0, j)),
        pl.BlockSpec((None, D, tf), lambda b, r, j: (lf, 0, j)),
        pl.BlockSpec((None, W, tf), lambda b, r, j: (lf, 0, j)),
        pl.BlockSpec((None, 1, tf), lambda b, r, j: (lf, 0, j)),
        pl.BlockSpec((None, tf, D), lambda b, r, j: (lf, j, 0)),
        pl.BlockSpec((S, SUBLANES, F), lambda b, r, j: (b, 0, 0)),
    ]
    y, tails = pl.pallas_call(
        functools.partial(_ffn_kernel, n_act=len(acts), S=S, R=R, W=W, nj=nj, cr=cr, tn=tn),
        grid=(B // S, nr, nj),
        in_specs=in_specs,
        out_specs=[
            pl.BlockSpec((S, R, D), lambda b, r, j: (b, r, 0)),
            pl.BlockSpec((S, 1, W - 1, tf), lambda b, r, j: (b, r, 0, j)),
        ],
        out_shape=[jax.ShapeDtypeStruct((B, L, D), F32), jax.ShapeDtypeStruct((B, nr, W - 1, F), F32)],
        scratch_shapes=[
            pltpu.VMEM((S * R, D), BF16),
            pltpu.VMEM((S, R + SUBLANES, tf), F32),
            pltpu.VMEM((S * R, tf), F32),
            pltpu.VMEM((S * R, tf), BF16),
            pltpu.VMEM((nj, S, SUBLANES, tf), F32),
        ],
        compiler_params=_params(3), name="conv_ffn",
    )(*acts, w_out, x, g.reshape(1, D), w_up, w_gate, conv_w, conv_b.reshape(-1, 1, F), w_down, cache8)
    return y, tails[:, -1]


def _ret_kernel(q_ref, k_ref, v_ref, g_ref, cos_ref, sin_ref, s0_ref, o_ref, so_ref, S_ref, *, H, dk, dv, Lc):
    c = pl.program_id(1)

    @pl.when(c == 0)
    def _():
        S_ref[...] = s0_ref[0]

    idx = lax.broadcasted_iota(jnp.int32, (Lc, 1), 0).astype(F32)
    diff = (lax.broadcasted_iota(jnp.int32, (Lc, Lc), 0) - lax.broadcasted_iota(jnp.int32, (Lc, Lc), 1)).astype(F32)
    cos, sin = cos_ref[...], sin_ref[...]
    log_gamma = lambda h: math.log1p(-(2.0 ** (-5.0 - h)))

    def stage1(h):
        lg = log_gamma(h)
        q = q_ref[0, :, h * dk:(h + 1) * dk].astype(F32)
        k = k_ref[0, :, h * dk:(h + 1) * dk].astype(F32)
        qr = q * cos + pltpu.roll(q, dk // 2, 1) * sin
        kr = (k * cos + pltpu.roll(k, dk // 2, 1) * sin) * (dk ** -0.5)
        dec = jnp.where(diff >= 0, jnp.exp(lg * jnp.maximum(diff, 0.0)), 0.0)
        qb = qr.astype(BF16)
        s = _dot_nt(qb, kr.astype(BF16)) * dec
        return qb, kr, s

    def stage2(h, t):
        lg = log_gamma(h)
        qb, kr, s = t
        v = v_ref[0, :, h * dv:(h + 1) * dv]
        Sh = S_ref[h]
        o = _dot(s.astype(BF16), v) + _dot(qb, Sh.astype(BF16)) * jnp.exp((idx + 1.0) * lg)
        kd = kr * jnp.exp((Lc - 1.0 - idx) * lg)
        S_ref[h] = math.exp(Lc * lg) * Sh + _dot_tn(kd.astype(BF16), v)
        return o

    def stage3(h, o):
        on = o * lax.rsqrt(jnp.mean(o * o, -1, keepdims=True) + EPS)
        gg = g_ref[0, :, h * dv:(h + 1) * dv].astype(F32)
        o_ref[0, :, h * dv:(h + 1) * dv] = (on * (gg * jax.nn.sigmoid(gg))).astype(o_ref.dtype)

    t1, t2 = {}, {}
    for step in range(H + 2):
        if step < H:
            t1[step] = stage1(step)
        if 0 <= step - 1 < H:
            t2[step - 1] = stage2(step - 1, t1.pop(step - 1))
        if 0 <= step - 2 < H:
            stage3(step - 2, t2.pop(step - 2))

    @pl.when(c == pl.num_programs(1) - 1)
    def _():
        so_ref[0] = S_ref[...]


def _retention(q, k, v, g, cos, sin, S0, layer):
    B, L, _ = q.shape
    _, _, H, dk, dv = S0.shape
    Lc = min(RET_CHUNK, L)
    assert L % Lc == 0
    seq = lambda w: pl.BlockSpec((1, Lc, w), lambda b, c: (b, c, 0))
    tab = pl.BlockSpec((Lc, dk), lambda b, c: (c, 0))
    return pl.pallas_call(
        functools.partial(_ret_kernel, H=H, dk=dk, dv=dv, Lc=Lc),
        grid=(B, L // Lc),
        in_specs=[seq(H * dk), seq(H * dk), seq(H * dv), seq(H * dv), tab, tab,
                  pl.BlockSpec((None, 1, H, dk, dv), lambda b, c: (layer, b, 0, 0, 0))],
        out_specs=[seq(H * dv), pl.BlockSpec((1, H, dk, dv), lambda b, c: (b, 0, 0, 0))],
        out_shape=[jax.ShapeDtypeStruct((B, L, H * dv), BF16), jax.ShapeDtypeStruct(S0.shape[1:], F32)],
        scratch_shapes=[pltpu.VMEM((H, dk, dv), F32)],
        compiler_params=_params(2), name="retention",
    )(q, k, v, g, cos, sin, S0)


def _pool_kernel(u_ref, cache_ref, wp_ref, ps_ref, o_ref, ext_ref, *, tr, gd, pos0):
    r = pl.program_id(1)
    PB = POOL_HALO

    @pl.when(r == 0)
    def _():
        ext_ref[0:PB, :] = cache_ref[0]

    @pl.when(r > 0)
    def _():
        ext_ref[0:PB, :] = ext_ref[tr:tr + PB, :]

    ext_ref[PB:PB + tr, :] = u_ref[0]
    pos = pos0 + r * tr + lax.broadcasted_iota(jnp.int32, (tr, 1), 0)
    for j, w in enumerate(POOL_WINDOWS):
        lo, hi = j * gd, (j + 1) * gd
        acc = ext_ref[PB:PB + tr, lo:hi]
        for d in range(1, w):
            acc = acc + ext_ref[pl.ds(PB - d, tr), lo:hi]
        cnt = jnp.minimum(pos + 1, w).astype(F32)
        pooled = acc / cnt - u_ref[0, :, lo:hi]
        y = _dot(pooled.astype(BF16), wp_ref[j]) * ps_ref[:, lo:hi]
        o_ref[0, :, lo:hi] = y.astype(o_ref.dtype)


def _pool(u, cache, w_pool, pool_scale, pos0):
    B, L, P = u.shape
    G, gd, _ = w_pool.shape
    assert G == len(POOL_WINDOWS) and G * gd == P and cache.shape[1] == max(POOL_WINDOWS) - 1
    tr = min(POOL_ROWS, L)
    assert L % tr == 0 and tr >= POOL_HALO
    cache_h = jnp.pad(cache, ((0, 0), (POOL_HALO - cache.shape[1], 0), (0, 0)))
    return pl.pallas_call(
        functools.partial(_pool_kernel, tr=tr, gd=gd, pos0=pos0),
        grid=(B, L // tr),
        in_specs=[
            pl.BlockSpec((1, tr, P), lambda b, r: (b, r, 0)),
            pl.BlockSpec((1, POOL_HALO, P), lambda b, r: (b, 0, 0)),
            pl.BlockSpec((G, gd, gd), lambda b, r: (0, 0, 0)),
            pl.BlockSpec((1, P), lambda b, r: (0, 0)),
        ],
        out_specs=pl.BlockSpec((1, tr, P), lambda b, r: (b, r, 0)),
        out_shape=jax.ShapeDtypeStruct((B, L, P), BF16),
        scratch_shapes=[pltpu.VMEM((tr + POOL_HALO, P), F32)],
        compiler_params=_params(2), name="ms_pool",
    )(u, cache_h, w_pool, pool_scale.reshape(1, P))


def _mlstm_kernel(qk_ref, v_ref, og_ref, gt_ref, cache_ref, cw_ref, cb_ref, bias_ref, ng_ref,
                  C0_ref, n0_ref, m0_ref, y_ref, Co_ref, no_ref, mo_ref,
                  ext_ref, C_ref, n_ref, m_ref, *, H, dk, dv, Lc, W):
    c = pl.program_id(1)
    H8 = SUBLANES

    @pl.when(c == 0)
    def _():
        ext_ref[0:H8, :] = cache_ref[0]
        C_ref[...] = C0_ref[0]
        n_ref[...] = n0_ref[0]
        m_ref[...] = m0_ref[0]

    @pl.when(c > 0)
    def _():
        ext_ref[0:H8, :] = ext_ref[Lc:Lc + H8, :]

    ext_ref[H8:H8 + Lc, :] = qk_ref[0]

    def conv_silu(lo, hi):
        acc = None
        for t in range(W):
            term = ext_ref[pl.ds(H8 - (W - 1) + t, Lc), lo:hi] * cw_ref[t:t + 1, lo:hi]
            acc = term if acc is None else acc + term
        pre = acc + cb_ref[:, lo:hi]
        return pre * jax.nn.sigmoid(pre)

    gt = gt_ref[0] + bias_ref[...]
    lane = lax.broadcasted_iota(jnp.int32, gt.shape, 1)
    row = lax.broadcasted_iota(jnp.int32, gt.shape, 0)
    b = jax.nn.log_sigmoid(gt)
    sh = 1
    while sh < Lc:
        b = b + jnp.where(row >= sh, pltpu.roll(b, sh, 0), 0.0)
        sh *= 2
    A = jnp.where(lane < H, gt, b)
    AT = A.T
    causal = lax.broadcasted_iota(jnp.int32, (Lc, Lc), 0) >= lax.broadcasted_iota(jnp.int32, (Lc, Lc), 1)

    def stage1(h):
        i_col, b_col = A[:, h:h + 1], A[:, H + h:H + h + 1]
        i_row, b_row = AT[h:h + 1, :], AT[H + h:H + h + 1, :]
        m_prev = m_ref[h:h + 1, 0:1]
        dlog = jnp.where(causal, b_col - b_row + i_row, -jnp.inf)
        inter_log = b_col + m_prev
        m_t = jnp.maximum(inter_log, jnp.max(dlog, -1, keepdims=True))
        dw = jnp.exp(dlog - m_t)
        iw = jnp.exp(inter_log - m_t)
        q = conv_silu(h * dk, (h + 1) * dk)
        k = conv_silu((H + h) * dk, (H + h + 1) * dk) * (dk ** -0.5)
        qb = q.astype(BF16)
        s = _dot_nt(qb, k.astype(BF16)) * dw
        return dict(i_col=i_col, b_col=b_col, m_prev=m_prev, m_t=m_t, iw=iw, q=q, k=k, qb=qb, s=s)

    def stage2(h, t):
        v = v_ref[0, :, h * dv:(h + 1) * dv]
        Ch = C_ref[h]
        nh = n_ref[h:h + 1, :]
        s, iw, m_t, b_col = t['s'], t['iw'], t['m_t'], t['b_col']
        num = _dot(s.astype(BF16), v) + _dot(t['qb'], Ch.astype(BF16)) * iw
        den = jnp.sum(s, -1, keepdims=True) + jnp.sum(t['q'] * nh, -1, keepdims=True) * iw
        hh = num / jnp.maximum(jnp.abs(den), jnp.exp(-m_t))
        m_L, b_L = m_t[Lc - 1:Lc, :], b_col[Lc - 1:Lc, :]
        kw = t['k'] * jnp.exp(b_L - b_col + t['i_col'] - m_L)
        decay = jnp.exp(b_L + t['m_prev'] - m_L)
        C_ref[h] = decay * Ch + _dot_tn(kw.astype(BF16), v)
        n_ref[h:h + 1, :] = decay * nh + jnp.sum(kw, 0, keepdims=True)
        m_ref[h:h + 1, :] = jnp.broadcast_to(m_L, (1, m_ref.shape[1]))
        return hh

    def stage3(h, hh):
        hn = hh * lax.rsqrt(jnp.mean(hh * hh, -1, keepdims=True) + EPS) * ng_ref[:, h * dv:(h + 1) * dv]
        og = og_ref[0, :, h * dv:(h + 1) * dv].astype(F32)
        y_ref[0, :, h * dv:(h + 1) * dv] = (hn * jax.nn.sigmoid(og)).astype(y_ref.dtype)

    t1, t2 = {}, {}
    for step in range(H + 2):
        if step < H:
            t1[step] = stage1(step)
        if 0 <= step - 1 < H:
            t2[step - 1] = stage2(step - 1, t1.pop(step - 1))
        if 0 <= step - 2 < H:
            stage3(step - 2, t2.pop(step - 2))

    @pl.when(c == pl.num_programs(1) - 1)
    def _():
        Co_ref[0] = C_ref[...]
        no_ref[0] = n_ref[...]
        mo_ref[0] = m_ref[...]


def _mlstm(qk, v, og, gt, cache, conv_w, conv_b, gate_bias, norm_g, C0, layer, n0, m0):
    B, L, QK = qk.shape
    _, _, H, dk, dv = C0.shape
    W = conv_w.shape[0]
    NG = gt.shape[-1]
    Lc = min(ML_CHUNK, L)
    assert L % Lc == 0 and Lc >= SUBLANES >= W - 1 and QK == 2 * H * dk and H == SUBLANES and dk == LANES
    cache8 = jnp.pad(cache, ((0, 0), (SUBLANES - (W - 1), 0), (0, 0)))
    m0b = jnp.broadcast_to(m0[:, :, None], (B, H, LANES))
    seq = lambda w: pl.BlockSpec((1, Lc, w), lambda b, c: (b, c, 0))
    const = lambda shape: pl.BlockSpec(shape, lambda b, c: (0,) * len(shape))
    per_b = lambda shape: pl.BlockSpec((1,) + shape, lambda b, c: (b,) + (0,) * len(shape))
    y, C, n, m = pl.pallas_call(
        functools.partial(_mlstm_kernel, H=H, dk=dk, dv=dv, Lc=Lc, W=W),
        grid=(B, L // Lc),
        in_specs=[seq(QK), seq(H * dv), seq(H * dv), seq(NG), per_b((SUBLANES, QK)),
                  const((W, QK)), const((1, QK)), const((1, NG)), const((1, H * dv)),
                  pl.BlockSpec((None, 1, H, dk, dv), lambda b, c: (layer, b, 0, 0, 0)),
                  per_b((H, dk)), per_b((H, LANES))],
        out_specs=[seq(H * dv), per_b((H, dk, dv)), per_b((H, dk)), per_b((H, LANES))],
        out_shape=[jax.ShapeDtypeStruct((B, L, H * dv), BF16), jax.ShapeDtypeStruct(C0.shape[1:], F32),
                   jax.ShapeDtypeStruct(n0.shape, F32), jax.ShapeDtypeStruct((B, H, LANES), F32)],
        scratch_shapes=[pltpu.VMEM((Lc + SUBLANES, QK), F32), pltpu.VMEM((H, dk, dv), F32),
                        pltpu.VMEM((H, dk), F32), pltpu.VMEM((H, LANES), F32)],
        compiler_params=_params(2), name="mlstm",
    )(qk, v, og, gt, cache8, conv_w, conv_b.reshape(1, QK), gate_bias, norm_g.reshape(1, H * dv), C0, n0, m0b)
    return y, C, n, m[:, :, 0]


def _norm_kernel(x_ref, g_ref, o_ref):
    o_ref[...] = _rms(x_ref[...], g_ref[...])


def _final_norm(x, g):
    M, D = x.shape
    tm = min(ROW_TILE, M)
    assert M % tm == 0
    return pl.pallas_call(
        _norm_kernel, grid=(M // tm,),
        in_specs=[pl.BlockSpec((tm, D), lambda i: (i, 0)), pl.BlockSpec((1, D), lambda i: (0, 0))],
        out_specs=pl.BlockSpec((tm, D), lambda i: (i, 0)),
        out_shape=jax.ShapeDtypeStruct((M, D), F32),
        compiler_params=_params(1), name="final_norm",
    )(x, g.reshape(1, D))


def _rotary_tables(pos, dk):
    half = dk // 2
    inv = 1.0 / (10000.0 ** jnp.linspace(0.0, 1.0, half, dtype=F32))
    ang = pos.astype(F32)[:, None] * inv[None, :]
    cos, sin = jnp.cos(ang), jnp.sin(ang)
    return jnp.concatenate([cos, cos], -1), jnp.concatenate([-sin, sin], -1)


def _trunk(x, pos0, ret_S, pool_buf, ml_C, ml_n, ml_m, ml_conv, ffn_buf, p):
    B, L, D = x.shape
    depth = p['norm_mix'].shape[0]
    _, _, RH, rdk, rdv = ret_S.shape
    _, _, MH, mdk, mdv = ml_C.shape
    PW = pool_buf.shape[-1]
    qw, rw, mqk, mv = RH * rdk, RH * rdv, MH * mdk, MH * mdv
    assert L >= pool_buf.shape[2] and L >= ml_conv.shape[2]
    assert qw == rw == PW and p['w_in_a'].shape[2] == 2 * qw + 2 * rw + PW
    assert mv == mqk * 2 and p['w_in_c'].shape[2] >= 2 * mqk + 2 * mv + LANES and 2 * MH <= LANES
    cos, sin = _rotary_tables(pos0 + jnp.arange(L, dtype=jnp.int32), rdk)
    sh = lambda t: t.reshape(B, L, -1)
    o_S, o_pool, o_C, o_n, o_m, o_conv, o_ffn = [], [], [], [], [], [], []
    for l in range(depth):
        x2 = x.reshape(B * L, D)
        if l % 2 == 0:
            a = l // 2
            q, k, v, g, u = _proj(x2, p['norm_mix'][l], p['w_in_a'], a, qw, [BF16, BF16, BF16, BF16, F32])
            ret, S = _retention(sh(q), sh(k), sh(v), sh(g), cos, sin, ret_S, a)
            u3 = sh(u)
            pool = _pool(u3, pool_buf[a], p['w_pool'][a], p['pool_scale'][a], pos0)
            acts, w_out, lo = [ret.reshape(B * L, rw), pool.reshape(B * L, PW)], p['w_out_a'], a
            o_S.append(S)
            o_pool.append(u3[:, L - pool_buf.shape[2]:])
        else:
            c = l // 2
            qk, v, og, gt = _proj(x2, p['norm_mix'][l], p['w_in_c'], c, mv, [F32, BF16, BF16], small_cols=LANES)
            gate_bias = jnp.pad(jnp.concatenate([p['b_i'][c], p['b_f'][c]]), (0, LANES - 2 * MH)).reshape(1, LANES)
            qk3 = sh(qk)
            y, C, n, m = _mlstm(qk3, sh(v), sh(og), sh(gt), ml_conv[c], p['conv_c_w'][c], p['conv_c_b'][c],
                                gate_bias, p['ml_norm'][c], ml_C, c, ml_n[c], ml_m[c])
            acts, w_out, lo = [y.reshape(B * L, mv)], p['w_out_c'], c
            o_C.append(C)
            o_n.append(n)
            o_m.append(m)
            o_conv.append(qk3[:, L - ml_conv.shape[2]:])
        x, fb = _ffn(acts, w_out, lo, x, p['norm_ffn'][l], p['w_up'], p['w_gate'], p['ffn_conv_w'],
                     p['ffn_conv_b'], p['w_down'], l, ffn_buf[l])
        o_ffn.append(fb)
    y = _final_norm(x.reshape(B * L, D), p['norm_final']).reshape(B, L, D)
    return (y, jnp.stack(o_S), jnp.stack(o_pool), jnp.stack(o_C), jnp.stack(o_n), jnp.stack(o_m),
            jnp.stack(o_conv), jnp.stack(o_ffn))


def kernel(x_prompt, x_sample, state_ret, cache_pool, state_mlstm_C, state_mlstm_n, state_mlstm_m, cache_mlstm_conv, cache_ffn_conv, norm_mix, norm_ffn, norm_final, w_in_a, w_pool, pool_scale, w_out_a, w_in_c, conv_c_w, conv_c_b, b_i, b_f, ml_norm, w_out_c, w_up, w_gate, ffn_conv_w, ffn_conv_b, w_down):
    bf = lambda w: w.astype(BF16)
    gate_pad = -w_in_c.shape[2] % LANES
    w_in_c_p = jnp.pad(bf(w_in_c), ((0, 0), (0, 0), (0, gate_pad)))
    p = dict(norm_mix=norm_mix, norm_ffn=norm_ffn, norm_final=norm_final, w_in_a=bf(w_in_a), w_pool=bf(w_pool),
             pool_scale=pool_scale, w_out_a=bf(w_out_a), w_in_c=w_in_c_p, conv_c_w=conv_c_w, conv_c_b=conv_c_b,
             b_i=b_i, b_f=b_f, ml_norm=ml_norm, w_out_c=bf(w_out_c), w_up=bf(w_up), w_gate=bf(w_gate),
             ffn_conv_w=ffn_conv_w, ffn_conv_b=ffn_conv_b, w_down=bf(w_down))
    B0 = x_prompt.shape[0]
    zeros = lambda a: jnp.zeros((a.shape[0], B0) + a.shape[2:], F32)
    prompt = _trunk(x_prompt, 0, zeros(state_ret), zeros(cache_pool), zeros(state_mlstm_C), zeros(state_mlstm_n),
                    zeros(state_mlstm_m), zeros(cache_mlstm_conv), zeros(cache_ffn_conv), p)
    sample = _trunk(x_sample, PAST_LEN, state_ret, cache_pool, state_mlstm_C, state_mlstm_n, state_mlstm_m,
                    cache_mlstm_conv, cache_ffn_conv, p)
    return (prompt[0], sample[0]) + tuple(prompt[1:]) + tuple(sample[1:])
```

```python
import functools
import math

import jax
import jax.numpy as jnp
from jax import lax
from jax.experimental import pallas as pl
from jax.experimental.pallas import tpu as pltpu

EPS = 1e-6
PAST_LEN = 2048
POOL_WINDOWS = (2, 4, 8, 16)
F32 = jnp.float32
BF16 = jnp.bfloat16

SUBLANES = 8
LANES = 128
POOL_HALO = 16
VMEM_LIMIT = 56 * 1024 * 1024

ROW_TILE = 512
PROJ_ROWS = 1024
PROJ_COLS = 256
OUT_COLS = 512
FFN_COLS = 512
FFN_PARTS = 2
FFN_CHUNK_VREGS = 16
RET_CHUNK = 256
ML_CHUNK = 256
POOL_ROWS = 512


def _params(n_axes):
    return pltpu.CompilerParams(dimension_semantics=("arbitrary",) * n_axes, vmem_limit_bytes=VMEM_LIMIT)


def _rms(x, g):
    return x * lax.rsqrt(jnp.mean(x * x, -1, keepdims=True) + EPS) * g


def _dot(a, b):
    return jnp.dot(a, b, preferred_element_type=F32)


def _dot_nt(a, b):
    return lax.dot_general(a, b, (((1,), (1,)), ((), ())), preferred_element_type=F32)


def _dot_tn(a, b):
    return lax.dot_general(a, b, (((0,), (0,)), ((), ())), preferred_element_type=F32)


def _proj_kernel(*refs, n_seg, has_small):
    x_ref, g_ref = refs[0], refs[1]
    k = 2
    w_refs = refs[k:k + n_seg]
    k += n_seg
    ws_ref = refs[k] if has_small else None
    k += has_small
    o_refs = refs[k:k + n_seg]
    k += n_seg
    os_ref = refs[k] if has_small else None
    k += has_small
    h_ref = refs[k]

    @pl.when(pl.program_id(1) == 0)
    def _():
        h_ref[...] = _rms(x_ref[...], g_ref[...]).astype(BF16)
        if has_small:
            os_ref[...] = _dot(h_ref[...], ws_ref[...])

    h = h_ref[...]
    for w_ref, o_ref in zip(w_refs, o_refs):
        o_ref[...] = _dot(h, w_ref[...]).astype(o_ref.dtype)


def _proj(x, g, w, layer, wseg, out_dtypes, small_cols=0):
    M, D = x.shape
    n_seg = len(out_dtypes)
    tm, tn = min(PROJ_ROWS, M), min(PROJ_COLS, wseg)
    assert M % tm == 0 and wseg % tn == 0 and w.shape[2] >= n_seg * wseg + small_cols
    has_small = small_cols > 0
    nt = wseg // tn
    in_specs = [pl.BlockSpec((tm, D), lambda i, j: (i, 0)), pl.BlockSpec((1, D), lambda i, j: (0, 0))]
    in_specs += [pl.BlockSpec((None, D, tn), functools.partial(lambda i, j, s: (layer, 0, s * nt + j), s=s))
                 for s in range(n_seg)]
    out_specs = [pl.BlockSpec((tm, tn), lambda i, j: (i, j)) for _ in range(n_seg)]
    out_shape = [jax.ShapeDtypeStruct((M, wseg), dt) for dt in out_dtypes]
    args = [x, g.reshape(1, D)] + [w] * n_seg
    if has_small:
        assert (n_seg * wseg) % small_cols == 0
        in_specs.append(pl.BlockSpec((None, D, small_cols), lambda i, j: (layer, 0, n_seg * wseg // small_cols)))
        out_specs.append(pl.BlockSpec((tm, small_cols), lambda i, j: (i, 0)))
        out_shape.append(jax.ShapeDtypeStruct((M, small_cols), F32))
        args.append(w)
    return pl.pallas_call(
        functools.partial(_proj_kernel, n_seg=n_seg, has_small=has_small),
        grid=(M // tm, nt),
        in_specs=in_specs, out_specs=out_specs, out_shape=out_shape,
        scratch_shapes=[pltpu.VMEM((tm, D), BF16)],
        compiler_params=_params(2), name="norm_proj",
    )(*args)


def _ffn_kernel(*refs, n_act, S, R, W, nj, cr, tn):
    a_refs = refs[:n_act]
    (wo_ref, x_ref, g_ref, wu_ref, wg_ref, cw_ref, cb_ref, wd_ref, cache_ref, o_ref, nc_ref,
     h_ref, ext_ref, up_ref, act_ref, carry_ref) = refs[n_act:]
    r, j = pl.program_id(1), pl.program_id(2)
    D = x_ref.shape[-1]
    tf = wu_ref.shape[1]
    H8 = SUBLANES

    @pl.when(j == 0)
    def _():
        @pl.when(r == 0)
        def _():
            for jj in range(nj):
                carry_ref[jj] = cache_ref[:, :, jj * tf:(jj + 1) * tf]

        for n0 in range(0, D, tn):
            acc = x_ref[:, :, n0:n0 + tn].reshape(S * R, tn)
            off = 0
            for a_ref in a_refs:
                kk = a_ref.shape[1]
                acc = acc + _dot(a_ref[...], wo_ref[off:off + kk, n0:n0 + tn])
                off += kk
            o_ref[:, :, n0:n0 + tn] = acc.reshape(S, R, tn)
        h_ref[...] = _rms(o_ref[...].reshape(S * R, D), g_ref[...]).astype(BF16)

    P = FFN_PARTS
    half = S * R // P
    hs = S // P if S > 1 else 1

    def up_dot(k):
        rows = slice(k * half, (k + 1) * half)
        up_ref[rows, :] = _dot(h_ref[rows, :], wu_ref[...])

    def gate_dot(k):
        rows = slice(k * half, (k + 1) * half)
        g2 = _dot(h_ref[rows, :], wg_ref[...])
        if S > 1:
            ext_ref[k * hs:(k + 1) * hs, H8:H8 + R, :] = g2.reshape(hs, R, tf)
        else:
            ext_ref[0, H8 + k * half:H8 + (k + 1) * half, :] = g2

    def chunk(c):
        s, r0 = divmod(c * cr, R)
        acc = None
        for t in range(W):
            term = ext_ref[s, pl.ds(r0 + H8 - (W - 1) + t, cr), :] * cw_ref[t:t + 1, :]
            acc = term if acc is None else acc + term
        rows = slice(s * R + r0, s * R + r0 + cr)
        act_ref[rows, :] = (jax.nn.gelu(acc + cb_ref[...]) * up_ref[rows, :]).astype(BF16)

    def down_dot(k):
        y = _dot(act_ref[k * half:(k + 1) * half, :], wd_ref[...])
        if S > 1:
            o_ref[k * hs:(k + 1) * hs] += y.reshape(hs, R, D)
        else:
            o_ref[0, k * half:(k + 1) * half, :] += y

    nch = half // cr

    def chunks(k, lo, hi):
        for c in range(k * nch + lo, k * nch + hi):
            chunk(c)

    ext_ref[:, 0:H8, :] = carry_ref[j]
    up_dot(0)
    gate_dot(0)
    for k in range(1, P):
        up_dot(k)
        chunks(k - 1, 0, nch // 2)
        gate_dot(k)
        chunks(k - 1, nch // 2, nch)
        if k >= 2:
            down_dot(k - 2)
    carry_ref[j] = ext_ref[:, R:R + H8, :]
    nc_ref[:, 0] = ext_ref[:, H8 + R - (W - 1):H8 + R, :]
    if P >= 2:
        down_dot(P - 2)
    chunks(P - 1, 0, nch)
    down_dot(P - 1)


def _ffn(acts, w_out, lo, x, g, w_up, w_gate, conv_w, conv_b, w_down, lf, cache):
    B, L, D = x.shape
    F = w_up.shape[2]
    W = conv_w.shape[1]
    K = w_out.shape[1]
    R = min(ROW_TILE, L)
    S = min(B, max(1, ROW_TILE // R))
    tf, tn = min(FFN_COLS, F), min(OUT_COLS, D)
    assert L % R == 0 and B % S == 0 and F % tf == 0 and D % tn == 0 and R % SUBLANES == 0 and R >= SUBLANES >= W - 1
    assert sum(a.shape[1] for a in acts) == K and (S == 1 or R == L)
    nj = F // tf
    cr = min(R, FFN_CHUNK_VREGS * SUBLANES * LANES // tf)
    assert R % cr == 0 and cr % (2 * SUBLANES) == 0 and (S * R // FFN_PARTS) % (2 * cr) == 0
    assert S == 1 or S % FFN_PARTS == 0
    cache8 = jnp.pad(cache, ((0, 0), (SUBLANES - (W - 1), 0), (0, 0)))
    nr = L // R
    in_specs = [pl.BlockSpec((S * R, a.shape[1]), lambda b, r, j: (b * nr + r, 0)) for a in acts]
    in_specs += [
        pl.BlockSpec((None, K, D), lambda b, r, j: (lo, 0, 0), pipeline_mode=pl.Buffered(1)),
        pl.BlockSpec((S, R, D), lambda b, r, j: (b, r, 0)),
        pl.BlockSpec((1, D), lambda b, r, j: (0, 0)),
        pl.BlockSpec((None, D, tf), lambda b, r, j: (lf, 0, j)),
        pl.BlockSpec((None, D, tf), lambda b, r, j: (lf, 0, j)),
        pl.BlockSpec((None, W, tf), lambda b, r, j: (lf, 0, j)),
        pl.BlockSpec((None, 1, tf), lambda b, r, j: (lf, 0, j)),
        pl.BlockSpec((None, tf, D), lambda b, r, j: (lf, j, 0)),
        pl.BlockSpec((S, SUBLANES, F), lambda b, r, j: (b, 0, 0)),
    ]
    y, tails = pl.pallas_call(
        functools.partial(_ffn_kernel, n_act=len(acts), S=S, R=R, W=W, nj=nj, cr=cr, tn=tn),
        grid=(B // S, nr, nj),
        in_specs=in_specs,
        out_specs=[
            pl.BlockSpec((S, R, D), lambda b, r, j: (b, r, 0)),
            pl.BlockSpec((S, 1, W - 1, tf), lambda b, r, j: (b, r, 0, j)),
        ],
        out_shape=[jax.ShapeDtypeStruct((B, L, D), F32), jax.ShapeDtypeStruct((B, nr, W - 1, F), F32)],
        scratch_shapes=[
            pltpu.VMEM((S * R, D), BF16),
            pltpu.VMEM((S, R + SUBLANES, tf), F32),
            pltpu.VMEM((S * R, tf), F32),
            pltpu.VMEM((S * R, tf), BF16),
            pltpu.VMEM((nj, S, SUBLANES, tf), F32),
        ],
        compiler_params=_params(3), name="conv_ffn",
    )(*acts, w_out, x, g.reshape(1, D), w_up, w_gate, conv_w, conv_b.reshape(-1, 1, F), w_down, cache8)
    return y, tails[:, -1]


def _ret_kernel(q_ref, k_ref, v_ref, g_ref, cos_ref, sin_ref, s0_ref, o_ref, so_ref, S_ref, dec_ref, *, H, dk, dv, Lc):
    c = pl.program_id(1)
    log_gamma = lambda h: math.log1p(-(2.0 ** (-5.0 - h)))

    @pl.when(jnp.logical_and(pl.program_id(0) == 0, c == 0))
    def _():
        diff = (lax.broadcasted_iota(jnp.int32, (Lc, Lc), 0) - lax.broadcasted_iota(jnp.int32, (Lc, Lc), 1)).astype(F32)
        for h in range(H):
            dec_ref[h] = jnp.where(diff >= 0, jnp.exp(log_gamma(h) * jnp.maximum(diff, 0.0)), 0.0)

    @pl.when(c == 0)
    def _():
        S_ref[...] = s0_ref[0]

    idx = lax.broadcasted_iota(jnp.int32, (Lc, 1), 0).astype(F32)
    cos, sin = cos_ref[...], sin_ref[...]

    def stage1(h):
        q = q_ref[0, :, h * dk:(h + 1) * dk].astype(F32)
        k = k_ref[0, :, h * dk:(h + 1) * dk].astype(F32)
        qr = q * cos + pltpu.roll(q, dk // 2, 1) * sin
        kr = (k * cos + pltpu.roll(k, dk // 2, 1) * sin) * (dk ** -0.5)
        qb = qr.astype(BF16)
        s = _dot_nt(qb, kr.astype(BF16)) * dec_ref[h]
        return qb, kr, s

    def stage2(h, t):
        lg = log_gamma(h)
        qb, kr, s = t
        v = v_ref[0, :, h * dv:(h + 1) * dv]
        Sh = S_ref[h]
        o = _dot(s.astype(BF16), v) + _dot(qb, Sh.astype(BF16)) * jnp.exp((idx + 1.0) * lg)
        kd = kr * jnp.exp((Lc - 1.0 - idx) * lg)
        S_ref[h] = math.exp(Lc * lg) * Sh + _dot_tn(kd.astype(BF16), v)
        return o

    def stage3(h, o):
        on = o * lax.rsqrt(jnp.mean(o * o, -1, keepdims=True) + EPS)
        gg = g_ref[0, :, h * dv:(h + 1) * dv].astype(F32)
        o_ref[0, :, h * dv:(h + 1) * dv] = (on * (gg * jax.nn.sigmoid(gg))).astype(o_ref.dtype)

    t1, t2 = {}, {}
    for step in range(H + 2):
        if step < H:
            t1[step] = stage1(step)
        if 0 <= step - 1 < H:
            t2[step - 1] = stage2(step - 1, t1.pop(step - 1))
        if 0 <= step - 2 < H:
            stage3(step - 2, t2.pop(step - 2))

    @pl.when(c == pl.num_programs(1) - 1)
    def _():
        so_ref[0] = S_ref[...]


def _retention(q, k, v, g, cos, sin, S0, layer):
    B, L, _ = q.shape
    _, _, H, dk, dv = S0.shape
    Lc = min(RET_CHUNK, L)
    assert L % Lc == 0
    seq = lambda w: pl.BlockSpec((1, Lc, w), lambda b, c: (b, c, 0))
    tab = pl.BlockSpec((Lc, dk), lambda b, c: (c, 0))
    return pl.pallas_call(
        functools.partial(_ret_kernel, H=H, dk=dk, dv=dv, Lc=Lc),
        grid=(B, L // Lc),
        in_specs=[seq(H * dk), seq(H * dk), seq(H * dv), seq(H * dv), tab, tab,
                  pl.BlockSpec((None, 1, H, dk, dv), lambda b, c: (layer, b, 0, 0, 0))],
        out_specs=[seq(H * dv), pl.BlockSpec((1, H, dk, dv), lambda b, c: (b, 0, 0, 0))],
        out_shape=[jax.ShapeDtypeStruct((B, L, H * dv), BF16), jax.ShapeDtypeStruct(S0.shape[1:], F32)],
        scratch_shapes=[pltpu.VMEM((H, dk, dv), F32), pltpu.VMEM((H, Lc, Lc), F32)],
        compiler_params=_params(2), name="retention",
    )(q, k, v, g, cos, sin, S0)


def _pool_kernel(u_ref, cache_ref, wp_ref, ps_ref, o_ref, ext_ref, *, tr, gd, pos0):
    r = pl.program_id(1)
    PB = POOL_HALO

    @pl.when(r == 0)
    def _():
        ext_ref[0:PB, :] = cache_ref[0]

    @pl.when(r > 0)
    def _():
        ext_ref[0:PB, :] = ext_ref[tr:tr + PB, :]

    ext_ref[PB:PB + tr, :] = u_ref[0]
    pos = pos0 + r * tr + lax.broadcasted_iota(jnp.int32, (tr, 1), 0)
    for j, w in enumerate(POOL_WINDOWS):
        lo, hi = j * gd, (j + 1) * gd
        acc = ext_ref[PB:PB + tr, lo:hi]
        for d in range(1, w):
            acc = acc + ext_ref[pl.ds(PB - d, tr), lo:hi]
        cnt = jnp.minimum(pos + 1, w).astype(F32)
        pooled = acc / cnt - u_ref[0, :, lo:hi]
        y = _dot(pooled.astype(BF16), wp_ref[j]) * ps_ref[:, lo:hi]
        o_ref[0, :, lo:hi] = y.astype(o_ref.dtype)


def _pool(u, cache, w_pool, pool_scale, pos0):
    B, L, P = u.shape
    G, gd, _ = w_pool.shape
    assert G == len(POOL_WINDOWS) and G * gd == P and cache.shape[1] == max(POOL_WINDOWS) - 1
    tr = min(POOL_ROWS, L)
    assert L % tr == 0 and tr >= POOL_HALO
    cache_h = jnp.pad(cache, ((0, 0), (POOL_HALO - cache.shape[1], 0), (0, 0)))
    return pl.pallas_call(
        functools.partial(_pool_kernel, tr=tr, gd=gd, pos0=pos0),
        grid=(B, L // tr),
        in_specs=[
            pl.BlockSpec((1, tr, P), lambda b, r: (b, r, 0)),
            pl.BlockSpec((1, POOL_HALO, P), lambda b, r: (b, 0, 0)),
            pl.BlockSpec((G, gd, gd), lambda b, r: (0, 0, 0)),
            pl.BlockSpec((1, P), lambda b, r: (0, 0)),
        ],
        out_specs=pl.BlockSpec((1, tr, P), lambda b, r: (b, r, 0)),
        out_shape=jax.ShapeDtypeStruct((B, L, P), BF16),
        scratch_shapes=[pltpu.VMEM((tr + POOL_HALO, P), F32)],
        compiler_params=_params(2), name="ms_pool",
    )(u, cache_h, w_pool, pool_scale.reshape(1, P))


def _mlstm_kernel(qk_ref, v_ref, og_ref, gt_ref, cache_ref, cw_ref, cb_ref, bias_ref, ng_ref,
                  C0_ref, n0_ref, m0_ref, y_ref, Co_ref, no_ref, mo_ref,
                  ext_ref, C_ref, n_ref, m_ref, *, H, dk, dv, Lc, W):
    c = pl.program_id(1)
    H8 = SUBLANES

    @pl.when(c == 0)
    def _():
        ext_ref[0:H8, :] = cache_ref[0]
        C_ref[...] = C0_ref[0]
        n_ref[...] = n0_ref[0]
        m_ref[...] = m0_ref[0]

    @pl.when(c > 0)
    def _():
        ext_ref[0:H8, :] = ext_ref[Lc:Lc + H8, :]

    ext_ref[H8:H8 + Lc, :] = qk_ref[0]

    def conv_silu(lo, hi):
        acc = None
        for t in range(W):
            term = ext_ref[pl.ds(H8 - (W - 1) + t, Lc), lo:hi] * cw_ref[t:t + 1, lo:hi]
            acc = term if acc is None else acc + term
        pre = acc + cb_ref[:, lo:hi]
        return pre * jax.nn.sigmoid(pre)

    gt = gt_ref[0] + bias_ref[...]
    lane = lax.broadcasted_iota(jnp.int32, gt.shape, 1)
    row = lax.broadcasted_iota(jnp.int32, gt.shape, 0)
    b = jax.nn.log_sigmoid(gt)
    sh = 1
    while sh < Lc:
        b = b + jnp.where(row >= sh, pltpu.roll(b, sh, 0), 0.0)
        sh *= 2
    A = jnp.where(lane < H, gt, b)
    AT = A.T
    causal = lax.broadcasted_iota(jnp.int32, (Lc, Lc), 0) >= lax.broadcasted_iota(jnp.int32, (Lc, Lc), 1)

    def stage1(h):
        i_col, b_col = A[:, h:h + 1], A[:, H + h:H + h + 1]
        i_row, b_row = AT[h:h + 1, :], AT[H + h:H + h + 1, :]
        m_prev = m_ref[h:h + 1, 0:1]
        dlog = jnp.where(causal, b_col - b_row + i_row, -jnp.inf)
        inter_log = b_col + m_prev
        m_t = jnp.maximum(inter_log, jnp.max(dlog, -1, keepdims=True))
        dw = jnp.exp(dlog - m_t)
        iw = jnp.exp(inter_log - m_t)
        q = conv_silu(h * dk, (h + 1) * dk)
        k = conv_silu((H + h) * dk, (H + h + 1) * dk) * (dk ** -0.5)
        qb = q.astype(BF16)
        s = _dot_nt(qb, k.astype(BF16)) * dw
        return dict(i_col=i_col, b_col=b_col, m_prev=m_prev, m_t=m_t, iw=iw, q=q, k=k, qb=qb, s=s)

    def stage2(h, t):
        v = v_ref[0, :, h * dv:(h + 1) * dv]
        Ch = C_ref[h]
        nh = n_ref[h:h + 1, :]
        s, iw, m_t, b_col = t['s'], t['iw'], t['m_t'], t['b_col']
        num = _dot(s.astype(BF16), v) + _dot(t['qb'], Ch.astype(BF16)) * iw
        den = jnp.sum(s, -1, keepdims=True) + jnp.sum(t['q'] * nh, -1, keepdims=True) * iw
        hh = num / jnp.maximum(jnp.abs(den), jnp.exp(-m_t))
        m_L, b_L = m_t[Lc - 1:Lc, :], b_col[Lc - 1:Lc, :]
        kw = t['k'] * jnp.exp(b_L - b_col + t['i_col'] - m_L)
        decay = jnp.exp(b_L + t['m_prev'] - m_L)
        C_ref[h] = decay * Ch + _dot_tn(kw.astype(BF16), v)
        n_ref[h:h + 1, :] = decay * nh + jnp.sum(kw, 0, keepdims=True)
        m_ref[h:h + 1, :] = jnp.broadcast_to(m_L, (1, m_ref.shape[1]))
        return hh

    def stage3(h, hh):
        hn = hh * lax.rsqrt(jnp.mean(hh * hh, -1, keepdims=True) + EPS) * ng_ref[:, h * dv:(h + 1) * dv]
        og = og_ref[0, :, h * dv:(h + 1) * dv].astype(F32)
        y_ref[0, :, h * dv:(h + 1) * dv] = (hn * jax.nn.sigmoid(og)).astype(y_ref.dtype)

    t1, t2 = {}, {}
    for step in range(H + 2):
        if step < H:
            t1[step] = stage1(step)
        if 0 <= step - 1 < H:
            t2[step - 1] = stage2(step - 1, t1.pop(step - 1))
        if 0 <= step - 2 < H:
            stage3(step - 2, t2.pop(step - 2))

    @pl.when(c == pl.num_programs(1) - 1)
    def _():
        Co_ref[0] = C_ref[...]
        no_ref[0] = n_ref[...]
        mo_ref[0] = m_ref[...]


def _mlstm(qk, v, og, gt, cache, conv_w, conv_b, gate_bias, norm_g, C0, layer, n0, m0):
    B, L, QK = qk.shape
    _, _, H, dk, dv = C0.shape
    W = conv_w.shape[0]
    NG = gt.shape[-1]
    Lc = min(ML_CHUNK, L)
    assert L % Lc == 0 and Lc >= SUBLANES >= W - 1 and QK == 2 * H * dk and H == SUBLANES and dk == LANES
    cache8 = jnp.pad(cache, ((0, 0), (SUBLANES - (W - 1), 0), (0, 0)))
    m0b = jnp.broadcast_to(m0[:, :, None], (B, H, LANES))
    seq = lambda w: pl.BlockSpec((1, Lc, w), lambda b, c: (b, c, 0))
    const = lambda shape: pl.BlockSpec(shape, lambda b, c: (0,) * len(shape))
    per_b = lambda shape: pl.BlockSpec((1,) + shape, lambda b, c: (b,) + (0,) * len(shape))
    y, C, n, m = pl.pallas_call(
        functools.partial(_mlstm_kernel, H=H, dk=dk, dv=dv, Lc=Lc, W=W),
        grid=(B, L // Lc),
        in_specs=[seq(QK), seq(H * dv), seq(H * dv), seq(NG), per_b((SUBLANES, QK)),
                  const((W, QK)), const((1, QK)), const((1, NG)), const((1, H * dv)),
                  pl.BlockSpec((None, 1, H, dk, dv), lambda b, c: (layer, b, 0, 0, 0)),
                  per_b((H, dk)), per_b((H, LANES))],
        out_specs=[seq(H * dv), per_b((H, dk, dv)), per_b((H, dk)), per_b((H, LANES))],
        out_shape=[jax.ShapeDtypeStruct((B, L, H * dv), BF16), jax.ShapeDtypeStruct(C0.shape[1:], F32),
                   jax.ShapeDtypeStruct(n0.shape, F32), jax.ShapeDtypeStruct((B, H, LANES), F32)],
        scratch_shapes=[pltpu.VMEM((Lc + SUBLANES, QK), F32), pltpu.VMEM((H, dk, dv), F32),
                        pltpu.VMEM((H, dk), F32), pltpu.VMEM((H, LANES), F32)],
        compiler_params=_params(2), name="mlstm",
    )(qk, v, og, gt, cache8, conv_w, conv_b.reshape(1, QK), gate_bias, norm_g.reshape(1, H * dv), C0, n0, m0b)
    return y, C, n, m[:, :, 0]


def _norm_kernel(x_ref, g_ref, o_ref):
    o_ref[...] = _rms(x_ref[...], g_ref[...])


def _final_norm(x, g):
    M, D = x.shape
    tm = min(ROW_TILE, M)
    assert M % tm == 0
    return pl.pallas_call(
        _norm_kernel, grid=(M // tm,),
        in_specs=[pl.BlockSpec((tm, D), lambda i: (i, 0)), pl.BlockSpec((1, D), lambda i: (0, 0))],
        out_specs=pl.BlockSpec((tm, D), lambda i: (i, 0)),
        out_shape=jax.ShapeDtypeStruct((M, D), F32),
        compiler_params=_params(1), name="final_norm",
    )(x, g.reshape(1, D))


def _rotary_tables(pos, dk):
    half = dk // 2
    inv = 1.0 / (10000.0 ** jnp.linspace(0.0, 1.0, half, dtype=F32))
    ang = pos.astype(F32)[:, None] * inv[None, :]
    cos, sin = jnp.cos(ang), jnp.sin(ang)
    return jnp.concatenate([cos, cos], -1), jnp.concatenate([-sin, sin], -1)


def _trunk(x, pos0, ret_S, pool_buf, ml_C, ml_n, ml_m, ml_conv, ffn_buf, p):
    B, L, D = x.shape
    depth = p['norm_mix'].shape[0]
    _, _, RH, rdk, rdv = ret_S.shape
    _, _, MH, mdk, mdv = ml_C.shape
    PW = pool_buf.shape[-1]
    qw, rw, mqk, mv = RH * rdk, RH * rdv, MH * mdk, MH * mdv
    assert L >= pool_buf.shape[2] and L >= ml_conv.shape[2]
    assert qw == rw == PW and p['w_in_a'].shape[2] == 2 * qw + 2 * rw + PW
    assert mv == mqk * 2 and p['w_in_c'].shape[2] >= 2 * mqk + 2 * mv + LANES and 2 * MH <= LANES
    cos, sin = _rotary_tables(pos0 + jnp.arange(L, dtype=jnp.int32), rdk)
    sh = lambda t: t.reshape(B, L, -1)
    o_S, o_pool, o_C, o_n, o_m, o_conv, o_ffn = [], [], [], [], [], [], []
    for l in range(depth):
        x2 = x.reshape(B * L, D)
        if l % 2 == 0:
            a = l // 2
            q, k, v, g, u = _proj(x2, p['norm_mix'][l], p['w_in_a'], a, qw, [BF16, BF16, BF16, BF16, F32])
            ret, S = _retention(sh(q), sh(k), sh(v), sh(g), cos, sin, ret_S, a)
            u3 = sh(u)
            pool = _pool(u3, pool_buf[a], p['w_pool'][a], p['pool_scale'][a], pos0)
            acts, w_out, lo = [ret.reshape(B * L, rw), pool.reshape(B * L, PW)], p['w_out_a'], a
            o_S.append(S)
            o_pool.append(u3[:, L - pool_buf.shape[2]:])
        else:
            c = l // 2
            qk, v, og, gt = _proj(x2, p['norm_mix'][l], p['w_in_c'], c, mv, [F32, BF16, BF16], small_cols=LANES)
            gate_bias = jnp.pad(jnp.concatenate([p['b_i'][c], p['b_f'][c]]), (0, LANES - 2 * MH)).reshape(1, LANES)
            qk3 = sh(qk)
            y, C, n, m = _mlstm(qk3, sh(v), sh(og), sh(gt), ml_conv[c], p['conv_c_w'][c], p['conv_c_b'][c],
                                gate_bias, p['ml_norm'][c], ml_C, c, ml_n[c], ml_m[c])
            acts, w_out, lo = [y.reshape(B * L, mv)], p['w_out_c'], c
            o_C.append(C)
            o_n.append(n)
            o_m.append(m)
            o_conv.append(qk3[:, L - ml_conv.shape[2]:])
        x, fb = _ffn(acts, w_out, lo, x, p['norm_ffn'][l], p['w_up'], p['w_gate'], p['ffn_conv_w'],
                     p['ffn_conv_b'], p['w_down'], l, ffn_buf[l])
        o_ffn.append(fb)
    y = _final_norm(x.reshape(B * L, D), p['norm_final']).reshape(B, L, D)
    return (y, jnp.stack(o_S), jnp.stack(o_pool), jnp.stack(o_C), jnp.stack(o_n), jnp.stack(o_m),
            jnp.stack(o_conv), jnp.stack(o_ffn))


def kernel(x_prompt, x_sample, state_ret, cache_pool, state_mlstm_C, state_mlstm_n, state_mlstm_m, cache_mlstm_conv, cache_ffn_conv, norm_mix, norm_ffn, norm_final, w_in_a, w_pool, pool_scale, w_out_a, w_in_c, conv_c_w, conv_c_b, b_i, b_f, ml_norm, w_out_c, w_up, w_gate, ffn_conv_w, ffn_conv_b, w_down):
    bf = lambda w: w.astype(BF16)
    gate_pad = -w_in_c.shape[2] % LANES
    w_in_c_p = jnp.pad(bf(w_in_c), ((0, 0), (0, 0), (0, gate_pad)))
    p = dict(norm_mix=norm_mix, norm_ffn=norm_ffn, norm_final=norm_final, w_in_a=bf(w_in_a), w_pool=bf(w_pool),
             pool_scale=pool_scale, w_out_a=bf(w_out_a), w_in_c=w_in_c_p, conv_c_w=conv_c_w, conv_c_b=conv_c_b,
             b_i=b_i, b_f=b_f, ml_norm=ml_norm, w_out_c=bf(w_out_c), w_up=bf(w_up), w_gate=bf(w_gate),
             ffn_conv_w=ffn_conv_w, ffn_conv_b=ffn_conv_b, w_down=bf(w_down))
    B0 = x_prompt.shape[0]
    zeros = lambda a: jnp.zeros((a.shape[0], B0) + a.shape[2:], F32)
    prompt = _trunk(x_prompt, 0, zeros(state_ret), zeros(cache_pool), zeros(state_mlstm_C), zeros(state_mlstm_n),
                    zeros(state_mlstm_m), zeros(cache_mlstm_conv), zeros(cache_ffn_conv), p)
    sample = _trunk(x_sample, PAST_LEN, state_ret, cache_pool, state_mlstm_C, state_mlstm_n, state_mlstm_m,
                    cache_mlstm_conv, cache_ffn_conv, p)
    return (prompt[0], sample[0]) + tuple(prompt[1:]) + tuple(sample[1:])
```

```python
import functools
import math

import jax
import jax.numpy as jnp
from jax import lax
from jax.experimental import pallas as pl
from jax.experimental.pallas import tpu as pltpu

EPS = 1e-6
PAST_LEN = 2048
POOL_WINDOWS = (2, 4, 8, 16)
F32 = jnp.float32
BF16 = jnp.bfloat16

SUBLANES = 8
LANES = 128
POOL_HALO = 16
VMEM_LIMIT = 56 * 1024 * 1024

ROW_TILE = 512
PROJ_ROWS = 1024
PROJ_COLS = 256
OUT_COLS = 512
FFN_COLS = 512
FFN_PARTS = 2
FFN_CHUNK_VREGS = 16
RET_CHUNK = 256
ML_CHUNK = 256
POOL_ROWS = 512


def _params(n_axes):
    return pltpu.CompilerParams(dimension_semantics=("arbitrary",) * n_axes, vmem_limit_bytes=VMEM_LIMIT)


def _rms(x, g):
    return x * lax.rsqrt(jnp.mean(x * x, -1, keepdims=True) + EPS) * g


def _dot(a, b):
    return jnp.dot(a, b, preferred_element_type=F32)


def _dot_nt(a, b):
    return lax.dot_general(a, b, (((1,), (1,)), ((), ())), preferred_element_type=F32)


def _dot_tn(a, b):
    return lax.dot_general(a, b, (((0,), (0,)), ((), ())), preferred_element_type=F32)


def _causal_dwconv(win, w_ref, cols, n):
    W = w_ref.shape[0]
    acc = None
    for t in range(W):
        d = W - 1 - t
        rows = (pltpu.roll(win, d, 0) if d else win)[SUBLANES:SUBLANES + n]
        term = rows * w_ref[t:t + 1, cols]
        acc = term if acc is None else acc + term
    return acc


def _proj_kernel(*refs, n_seg, has_small):
    x_ref, g_ref = refs[0], refs[1]
    k = 2
    w_refs = refs[k:k + n_seg]
    k += n_seg
    ws_ref = refs[k] if has_small else None
    k += has_small
    o_refs = refs[k:k + n_seg]
    k += n_seg
    os_ref = refs[k] if has_small else None
    k += has_small
    h_ref = refs[k]

    @pl.when(pl.program_id(1) == 0)
    def _():
        h_ref[...] = _rms(x_ref[...], g_ref[...]).astype(BF16)
        if has_small:
            os_ref[...] = _dot(h_ref[...], ws_ref[...])

    h = h_ref[...]
    for w_ref, o_ref in zip(w_refs, o_refs):
        o_ref[...] = _dot(h, w_ref[...]).astype(o_ref.dtype)


def _proj(x, g, w, layer, wseg, out_dtypes, small_cols=0):
    M, D = x.shape
    n_seg = len(out_dtypes)
    tm, tn = min(PROJ_ROWS, M), min(PROJ_COLS, wseg)
    assert M % tm == 0 and wseg % tn == 0 and w.shape[2] >= n_seg * wseg + small_cols
    has_small = small_cols > 0
    nt = wseg // tn
    in_specs = [pl.BlockSpec((tm, D), lambda i, j: (i, 0)), pl.BlockSpec((1, D), lambda i, j: (0, 0))]
    in_specs += [pl.BlockSpec((None, D, tn), functools.partial(lambda i, j, s: (layer, 0, s * nt + j), s=s))
                 for s in range(n_seg)]
    out_specs = [pl.BlockSpec((tm, tn), lambda i, j: (i, j)) for _ in range(n_seg)]
    out_shape = [jax.ShapeDtypeStruct((M, wseg), dt) for dt in out_dtypes]
    args = [x, g.reshape(1, D)] + [w] * n_seg
    if has_small:
        assert (n_seg * wseg) % small_cols == 0
        in_specs.append(pl.BlockSpec((None, D, small_cols), lambda i, j: (layer, 0, n_seg * wseg // small_cols)))
        out_specs.append(pl.BlockSpec((tm, small_cols), lambda i, j: (i, 0)))
        out_shape.append(jax.ShapeDtypeStruct((M, small_cols), F32))
        args.append(w)
    return pl.pallas_call(
        functools.partial(_proj_kernel, n_seg=n_seg, has_small=has_small),
        grid=(M // tm, nt),
        in_specs=in_specs, out_specs=out_specs, out_shape=out_shape,
        scratch_shapes=[pltpu.VMEM((tm, D), BF16)],
        compiler_params=_params(2), name="norm_proj",
    )(*args)


def _ffn_kernel(*refs, n_act, S, R, W, nj, cr, tn):
    a_refs = refs[:n_act]
    (wo_ref, x_ref, g_ref, wu_ref, wg_ref, cw_ref, cb_ref, wd_ref, cache_ref, o_ref, nc_ref,
     h_ref, ext_ref, up_ref, act_ref, carry_ref) = refs[n_act:]
    r, j = pl.program_id(1), pl.program_id(2)
    D = x_ref.shape[-1]
    tf = wu_ref.shape[1]
    H8 = SUBLANES

    @pl.when(j == 0)
    def _():
        @pl.when(r == 0)
        def _():
            for jj in range(nj):
                carry_ref[jj] = cache_ref[:, :, jj * tf:(jj + 1) * tf]

        for n0 in range(0, D, tn):
            acc = x_ref[:, :, n0:n0 + tn].reshape(S * R, tn)
            off = 0
            for a_ref in a_refs:
                kk = a_ref.shape[1]
                acc = acc + _dot(a_ref[...], wo_ref[off:off + kk, n0:n0 + tn])
                off += kk
            o_ref[:, :, n0:n0 + tn] = acc.reshape(S, R, tn)
        h_ref[...] = _rms(o_ref[...].reshape(S * R, D), g_ref[...]).astype(BF16)

    P = FFN_PARTS
    half = S * R // P
    hs = S // P if S > 1 else 1

    def up_dot(k):
        rows = slice(k * half, (k + 1) * half)
        up_ref[rows, :] = _dot(h_ref[rows, :], wu_ref[...])

    def gate_dot(k):
        rows = slice(k * half, (k + 1) * half)
        g2 = _dot(h_ref[rows, :], wg_ref[...])
        if S > 1:
            ext_ref[k * hs:(k + 1) * hs, H8:H8 + R, :] = g2.reshape(hs, R, tf)
        else:
            ext_ref[0, H8 + k * half:H8 + (k + 1) * half, :] = g2

    def chunk(c):
        s, r0 = divmod(c * cr, R)
        acc = _causal_dwconv(ext_ref[s, r0:r0 + H8 + cr, :], cw_ref, slice(None), cr)
        rows = slice(s * R + r0, s * R + r0 + cr)
        act_ref[rows, :] = (jax.nn.gelu(acc + cb_ref[...]) * up_ref[rows, :]).astype(BF16)

    def down_dot(k):
        y = _dot(act_ref[k * half:(k + 1) * half, :], wd_ref[...])
        if S > 1:
            o_ref[k * hs:(k + 1) * hs] += y.reshape(hs, R, D)
        else:
            o_ref[0, k * half:(k + 1) * half, :] += y

    nch = half // cr

    def chunks(k, lo, hi):
        for c in range(k * nch + lo, k * nch + hi):
            chunk(c)

    ext_ref[:, 0:H8, :] = carry_ref[j]
    up_dot(0)
    gate_dot(0)
    for k in range(1, P):
        up_dot(k)
        chunks(k - 1, 0, nch // 2)
        gate_dot(k)
        chunks(k - 1, nch // 2, nch)
        if k >= 2:
            down_dot(k - 2)
    carry_ref[j] = ext_ref[:, R:R + H8, :]
    nc_ref[:, 0] = ext_ref[:, H8 + R - (W - 1):H8 + R, :]
    if P >= 2:
        down_dot(P - 2)
    chunks(P - 1, 0, nch)
    down_dot(P - 1)


def _ffn(acts, w_out, lo, x, g, w_up, w_gate, conv_w, conv_b, w_down, lf, cache):
    B, L, D = x.shape
    F = w_up.shape[2]
    W = conv_w.shape[1]
    K = w_out.shape[1]
    R = min(ROW_TILE, L)
    S = min(B, max(1, ROW_TILE // R))
    tf, tn = min(FFN_COLS, F), min(OUT_COLS, D)
    assert L % R == 0 and B % S == 0 and F % tf == 0 and D % tn == 0 and R % SUBLANES == 0 and R >= SUBLANES >= W - 1
    assert sum(a.shape[1] for a in acts) == K and (S == 1 or R == L)
    nj = F // tf
    cr = min(R, FFN_CHUNK_VREGS * SUBLANES * LANES // tf)
    assert R % cr == 0 and cr % (2 * SUBLANES) == 0 and (S * R // FFN_PARTS) % (2 * cr) == 0
    assert S == 1 or S % FFN_PARTS == 0
    cache8 = jnp.pad(cache, ((0, 0), (SUBLANES - (W - 1), 0), (0, 0)))
    nr = L // R
    in_specs = [pl.BlockSpec((S * R, a.shape[1]), lambda b, r, j: (b * nr + r, 0)) for a in acts]
    in_specs += [
        pl.BlockSpec((None, K, D), lambda b, r, j: (lo, 0, 0), pipeline_mode=pl.Buffered(1)),
        pl.BlockSpec((S, R, D), lambda b, r, j: (b, r, 0)),
        pl.BlockSpec((1, D), lambda b, r, j: (0, 0)),
        pl.BlockSpec((None, D, tf), lambda b, r, j: (lf, 0, j)),
        pl.BlockSpec((None, D, tf), lambda b, r, j: (lf, 0, j)),
        pl.BlockSpec((None, W, tf), lambda b, r, j: (lf, 0, j)),
        pl.BlockSpec((None, 1, tf), lambda b, r, j: (lf, 0, j)),
        pl.BlockSpec((None, tf, D), lambda b, r, j: (lf, j, 0)),
        pl.BlockSpec((S, SUBLANES, F), lambda b, r, j: (b, 0, 0)),
    ]
    y, tails = pl.pallas_call(
        functools.partial(_ffn_kernel, n_act=len(acts), S=S, R=R, W=W, nj=nj, cr=cr, tn=tn),
        grid=(B // S, nr, nj),
        in_specs=in_specs,
        out_specs=[
            pl.BlockSpec((S, R, D), lambda b, r, j: (b, r, 0)),
            pl.BlockSpec((S, 1, W - 1, tf), lambda b, r, j: (b, r, 0, j)),
        ],
        out_shape=[jax.ShapeDtypeStruct((B, L, D), F32), jax.ShapeDtypeStruct((B, nr, W - 1, F), F32)],
        scratch_shapes=[
            pltpu.VMEM((S * R, D), BF16),
            pltpu.VMEM((S, R + SUBLANES, tf), F32),
            pltpu.VMEM((S * R, tf), F32),
            pltpu.VMEM((S * R, tf), BF16),
            pltpu.VMEM((nj, S, SUBLANES, tf), F32),
        ],
        compiler_params=_params(3), name="conv_ffn",
    )(*acts, w_out, x, g.reshape(1, D), w_up, w_gate, conv_w, conv_b.reshape(-1, 1, F), w_down, cache8)
    return y, tails[:, -1]


def _ret_kernel(q_ref, k_ref, v_ref, g_ref, cos_ref, sin_ref, s0_ref, o_ref, so_ref, S_ref, dec_ref, *, H, dk, dv, Lc):
    c = pl.program_id(1)
    log_gamma = lambda h: math.log1p(-(2.0 ** (-5.0 - h)))

    @pl.when(jnp.logical_and(pl.program_id(0) == 0, c == 0))
    def _():
        diff = (lax.broadcasted_iota(jnp.int32, (Lc, Lc), 0) - lax.broadcasted_iota(jnp.int32, (Lc, Lc), 1)).astype(F32)
        for h in range(H):
            dec_ref[h] = jnp.where(diff >= 0, jnp.exp(log_gamma(h) * jnp.maximum(diff, 0.0)), 0.0)

    @pl.when(c == 0)
    def _():
        S_ref[...] = s0_ref[0]

    idx = lax.broadcasted_iota(jnp.int32, (Lc, 1), 0).astype(F32)
    cos, sin = cos_ref[...], sin_ref[...]

    def stage1(h):
        q = q_ref[0, :, h * dk:(h + 1) * dk].astype(F32)
        k = k_ref[0, :, h * dk:(h + 1) * dk].astype(F32)
        qr = q * cos + pltpu.roll(q, dk // 2, 1) * sin
        kr = (k * cos + pltpu.roll(k, dk // 2, 1) * sin) * (dk ** -0.5)
        qb = qr.astype(BF16)
        s = _dot_nt(qb, kr.astype(BF16)) * dec_ref[h]
        return qb, kr, s

    def stage2(h, t):
        lg = log_gamma(h)
        qb, kr, s = t
        v = v_ref[0, :, h * dv:(h + 1) * dv]
        Sh = S_ref[h]
        o = _dot(s.astype(BF16), v) + _dot(qb, Sh.astype(BF16)) * jnp.exp((idx + 1.0) * lg)
        kd = kr * jnp.exp((Lc - 1.0 - idx) * lg)
        S_ref[h] = math.exp(Lc * lg) * Sh + _dot_tn(kd.astype(BF16), v)
        return o

    def stage3(h, o):
        on = o * lax.rsqrt(jnp.mean(o * o, -1, keepdims=True) + EPS)
        gg = g_ref[0, :, h * dv:(h + 1) * dv].astype(F32)
        o_ref[0, :, h * dv:(h + 1) * dv] = (on * (gg * jax.nn.sigmoid(gg))).astype(o_ref.dtype)

    t1, t2 = {}, {}
    for step in range(H + 2):
        if step < H:
            t1[step] = stage1(step)
        if 0 <= step - 1 < H:
            t2[step - 1] = stage2(step - 1, t1.pop(step - 1))
        if 0 <= step - 2 < H:
            stage3(step - 2, t2.pop(step - 2))

    @pl.when(c == pl.num_programs(1) - 1)
    def _():
        so_ref[0] = S_ref[...]


def _retention(q, k, v, g, cos, sin, S0, layer):
    B, L, _ = q.shape
    _, _, H, dk, dv = S0.shape
    Lc = min(RET_CHUNK, L)
    assert L % Lc == 0
    seq = lambda w: pl.BlockSpec((1, Lc, w), lambda b, c: (b, c, 0))
    tab = pl.BlockSpec((Lc, dk), lambda b, c: (c, 0))
    return pl.pallas_call(
        functools.partial(_ret_kernel, H=H, dk=dk, dv=dv, Lc=Lc),
        grid=(B, L // Lc),
        in_specs=[seq(H * dk), seq(H * dk), seq(H * dv), seq(H * dv), tab, tab,
                  pl.BlockSpec((None, 1, H, dk, dv), lambda b, c: (layer, b, 0, 0, 0))],
        out_specs=[seq(H * dv), pl.BlockSpec((1, H, dk, dv), lambda b, c: (b, 0, 0, 0))],
        out_shape=[jax.ShapeDtypeStruct((B, L, H * dv), BF16), jax.ShapeDtypeStruct(S0.shape[1:], F32)],
        scratch_shapes=[pltpu.VMEM((H, dk, dv), F32), pltpu.VMEM((H, Lc, Lc), F32)],
        compiler_params=_params(2), name="retention",
    )(q, k, v, g, cos, sin, S0)


def _pool_kernel(u_ref, cache_ref, wp_ref, ps_ref, o_ref, ext_ref, *, tr, gd, pos0):
    r = pl.program_id(1)
    PB = POOL_HALO

    @pl.when(r == 0)
    def _():
        ext_ref[0:PB, :] = cache_ref[0]

    @pl.when(r > 0)
    def _():
        ext_ref[0:PB, :] = ext_ref[tr:tr + PB, :]

    ext_ref[PB:PB + tr, :] = u_ref[0]
    pos = pos0 + r * tr + lax.broadcasted_iota(jnp.int32, (tr, 1), 0)
    for j, w in enumerate(POOL_WINDOWS):
        lo, hi = j * gd, (j + 1) * gd
        acc = ext_ref[PB:PB + tr, lo:hi]
        for d in range(1, w):
            acc = acc + ext_ref[pl.ds(PB - d, tr), lo:hi]
        cnt = jnp.minimum(pos + 1, w).astype(F32)
        pooled = acc / cnt - u_ref[0, :, lo:hi]
        y = _dot(pooled.astype(BF16), wp_ref[j]) * ps_ref[:, lo:hi]
        o_ref[0, :, lo:hi] = y.astype(o_ref.dtype)


def _pool(u, cache, w_pool, pool_scale, pos0):
    B, L, P = u.shape
    G, gd, _ = w_pool.shape
    assert G == len(POOL_WINDOWS) and G * gd == P and cache.shape[1] == max(POOL_WINDOWS) - 1
    tr = min(POOL_ROWS, L)
    assert L % tr == 0 and tr >= POOL_HALO
    cache_h = jnp.pad(cache, ((0, 0), (POOL_HALO - cache.shape[1], 0), (0, 0)))
    return pl.pallas_call(
        functools.partial(_pool_kernel, tr=tr, gd=gd, pos0=pos0),
        grid=(B, L // tr),
        in_specs=[
            pl.BlockSpec((1, tr, P), lambda b, r: (b, r, 0)),
            pl.BlockSpec((1, POOL_HALO, P), lambda b, r: (b, 0, 0)),
            pl.BlockSpec((G, gd, gd), lambda b, r: (0, 0, 0)),
            pl.BlockSpec((1, P), lambda b, r: (0, 0)),
        ],
        out_specs=pl.BlockSpec((1, tr, P), lambda b, r: (b, r, 0)),
        out_shape=jax.ShapeDtypeStruct((B, L, P), BF16),
        scratch_shapes=[pltpu.VMEM((tr + POOL_HALO, P), F32)],
        compiler_params=_params(2), name="ms_pool",
    )(u, cache_h, w_pool, pool_scale.reshape(1, P))


def _mlstm_kernel(qk_ref, v_ref, og_ref, gt_ref, cache_ref, cw_ref, cb_ref, bias_ref, ng_ref,
                  C0_ref, n0_ref, m0_ref, y_ref, Co_ref, no_ref, mo_ref,
                  ext_ref, C_ref, n_ref, m_ref, *, H, dk, dv, Lc, W):
    c = pl.program_id(1)
    H8 = SUBLANES

    @pl.when(c == 0)
    def _():
        ext_ref[0:H8, :] = cache_ref[0]
        C_ref[...] = C0_ref[0]
        n_ref[...] = n0_ref[0]
        m_ref[...] = m0_ref[0]

    @pl.when(c > 0)
    def _():
        ext_ref[0:H8, :] = ext_ref[Lc:Lc + H8, :]

    ext_ref[H8:H8 + Lc, :] = qk_ref[0]

    def conv_silu(lo, hi):
        pre = _causal_dwconv(ext_ref[:, lo:hi], cw_ref, slice(lo, hi), Lc) + cb_ref[:, lo:hi]
        return pre * jax.nn.sigmoid(pre)

    gt = gt_ref[0] + bias_ref[...]
    lane = lax.broadcasted_iota(jnp.int32, gt.shape, 1)
    row = lax.broadcasted_iota(jnp.int32, gt.shape, 0)
    b = jax.nn.log_sigmoid(gt)
    sh = 1
    while sh < Lc:
        b = b + jnp.where(row >= sh, pltpu.roll(b, sh, 0), 0.0)
        sh *= 2
    A = jnp.where(lane < H, gt, b)
    AT = A.T
    causal = lax.broadcasted_iota(jnp.int32, (Lc, Lc), 0) >= lax.broadcasted_iota(jnp.int32, (Lc, Lc), 1)

    def stage1(h):
        i_col, b_col = A[:, h:h + 1], A[:, H + h:H + h + 1]
        i_row, b_row = AT[h:h + 1, :], AT[H + h:H + h + 1, :]
        m_prev = m_ref[h:h + 1, 0:1]
        dlog = jnp.where(causal, b_col - b_row + i_row, -jnp.inf)
        inter_log = b_col + m_prev
        m_t = jnp.maximum(inter_log, jnp.max(dlog, -1, keepdims=True))
        dw = jnp.exp(dlog - m_t)
        iw = jnp.exp(inter_log - m_t)
        q = conv_silu(h * dk, (h + 1) * dk)
        k = conv_silu((H + h) * dk, (H + h + 1) * dk) * (dk ** -0.5)
        qb = q.astype(BF16)
        s = _dot_nt(qb, k.astype(BF16)) * dw
        return dict(i_col=i_col, b_col=b_col, m_prev=m_prev, m_t=m_t, iw=iw, q=q, k=k, qb=qb, s=s)

    def stage2(h, t):
        v = v_ref[0, :, h * dv:(h + 1) * dv]
        Ch = C_ref[h]
        nh = n_ref[h:h + 1, :]
        s, iw, m_t, b_col = t['s'], t['iw'], t['m_t'], t['b_col']
        num = _dot(s.astype(BF16), v) + _dot(t['qb'], Ch.astype(BF16)) * iw
        den = jnp.sum(s, -1, keepdims=True) + jnp.sum(t['q'] * nh, -1, keepdims=True) * iw
        hh = num / jnp.maximum(jnp.abs(den), jnp.exp(-m_t))
        m_L, b_L = m_t[Lc - 1:Lc, :], b_col[Lc - 1:Lc, :]
        kw = t['k'] * jnp.exp(b_L - b_col + t['i_col'] - m_L)
        decay = jnp.exp(b_L + t['m_prev'] - m_L)
        C_ref[h] = decay * Ch + _dot_tn(kw.astype(BF16), v)
        n_ref[h:h + 1, :] = decay * nh + jnp.sum(kw, 0, keepdims=True)
        m_ref[h:h + 1, :] = jnp.broadcast_to(m_L, (1, m_ref.shape[1]))
        return hh

    def stage3(h, hh):
        hn = hh * lax.rsqrt(jnp.mean(hh * hh, -1, keepdims=True) + EPS) * ng_ref[:, h * dv:(h + 1) * dv]
        og = og_ref[0, :, h * dv:(h + 1) * dv].astype(F32)
        y_ref[0, :, h * dv:(h + 1) * dv] = (hn * jax.nn.sigmoid(og)).astype(y_ref.dtype)

    t1, t2 = {}, {}
    for step in range(H + 2):
        if step < H:
            t1[step] = stage1(step)
        if 0 <= step - 1 < H:
            t2[step - 1] = stage2(step - 1, t1.pop(step - 1))
        if 0 <= step - 2 < H:
            stage3(step - 2, t2.pop(step - 2))

    @pl.when(c == pl.num_programs(1) - 1)
    def _():
        Co_ref[0] = C_ref[...]
        no_ref[0] = n_ref[...]
        mo_ref[0] = m_ref[...]


def _mlstm(qk, v, og, gt, cache, conv_w, conv_b, gate_bias, norm_g, C0, layer, n0, m0):
    B, L, QK = qk.shape
    _, _, H, dk, dv = C0.shape
    W = conv_w.shape[0]
    NG = gt.shape[-1]
    Lc = min(ML_CHUNK, L)
    assert L % Lc == 0 and Lc >= SUBLANES >= W - 1 and QK == 2 * H * dk and H == SUBLANES and dk == LANES
    cache8 = jnp.pad(cache, ((0, 0), (SUBLANES - (W - 1), 0), (0, 0)))
    m0b = jnp.broadcast_to(m0[:, :, None], (B, H, LANES))
    seq = lambda w: pl.BlockSpec((1, Lc, w), lambda b, c: (b, c, 0))
    const = lambda shape: pl.BlockSpec(shape, lambda b, c: (0,) * len(shape))
    per_b = lambda shape: pl.BlockSpec((1,) + shape, lambda b, c: (b,) + (0,) * len(shape))
    y, C, n, m = pl.pallas_call(
        functools.partial(_mlstm_kernel, H=H, dk=dk, dv=dv, Lc=Lc, W=W),
        grid=(B, L // Lc),
        in_specs=[seq(QK), seq(H * dv), seq(H * dv), seq(NG), per_b((SUBLANES, QK)),
                  const((W, QK)), const((1, QK)), const((1, NG)), const((1, H * dv)),
                  pl.BlockSpec((None, 1, H, dk, dv), lambda b, c: (layer, b, 0, 0, 0)),
                  per_b((H, dk)), per_b((H, LANES))],
        out_specs=[seq(H * dv), per_b((H, dk, dv)), per_b((H, dk)), per_b((H, LANES))],
        out_shape=[jax.ShapeDtypeStruct((B, L, H * dv), BF16), jax.ShapeDtypeStruct(C0.shape[1:], F32),
                   jax.ShapeDtypeStruct(n0.shape, F32), jax.ShapeDtypeStruct((B, H, LANES), F32)],
        scratch_shapes=[pltpu.VMEM((Lc + SUBLANES, QK), F32), pltpu.VMEM((H, dk, dv), F32),
                        pltpu.VMEM((H, dk), F32), pltpu.VMEM((H, LANES), F32)],
        compiler_params=_params(2), name="mlstm",
    )(qk, v, og, gt, cache8, conv_w, conv_b.reshape(1, QK), gate_bias, norm_g.reshape(1, H * dv), C0, n0, m0b)
    return y, C, n, m[:, :, 0]


def _norm_kernel(x_ref, g_ref, o_ref):
    o_ref[...] = _rms(x_ref[...], g_ref[...])


def _final_norm(x, g):
    M, D = x.shape
    tm = min(ROW_TILE, M)
    assert M % tm == 0
    return pl.pallas_call(
        _norm_kernel, grid=(M // tm,),
        in_specs=[pl.BlockSpec((tm, D), lambda i: (i, 0)), pl.BlockSpec((1, D), lambda i: (0, 0))],
        out_specs=pl.BlockSpec((tm, D), lambda i: (i, 0)),
        out_shape=jax.ShapeDtypeStruct((M, D), F32),
        compiler_params=_params(1), name="final_norm",
    )(x, g.reshape(1, D))


def _rotary_tables(pos, dk):
    half = dk // 2
    inv = 1.0 / (10000.0 ** jnp.linspace(0.0, 1.0, half, dtype=F32))
    ang = pos.astype(F32)[:, None] * inv[None, :]
    cos, sin = jnp.cos(ang), jnp.sin(ang)
    return jnp.concatenate([cos, cos], -1), jnp.concatenate([-sin, sin], -1)


def _trunk(x, pos0, ret_S, pool_buf, ml_C, ml_n, ml_m, ml_conv, ffn_buf, p):
    B, L, D = x.shape
    depth = p['norm_mix'].shape[0]
    _, _, RH, rdk, rdv = ret_S.shape
    _, _, MH, mdk, mdv = ml_C.shape
    PW = pool_buf.shape[-1]
    qw, rw, mqk, mv = RH * rdk, RH * rdv, MH * mdk, MH * mdv
    assert L >= pool_buf.shape[2] and L >= ml_conv.shape[2]
    assert qw == rw == PW and p['w_in_a'].shape[2] == 2 * qw + 2 * rw + PW
    assert mv == mqk * 2 and p['w_in_c'].shape[2] >= 2 * mqk + 2 * mv + LANES and 2 * MH <= LANES
    cos, sin = _rotary_tables(pos0 + jnp.arange(L, dtype=jnp.int32), rdk)
    sh = lambda t: t.reshape(B, L, -1)
    o_S, o_pool, o_C, o_n, o_m, o_conv, o_ffn = [], [], [], [], [], [], []
    for l in range(depth):
        x2 = x.reshape(B * L, D)
        if l % 2 == 0:
            a = l // 2
            q, k, v, g, u = _proj(x2, p['norm_mix'][l], p['w_in_a'], a, qw, [BF16, BF16, BF16, BF16, F32])
            ret, S = _retention(sh(q), sh(k), sh(v), sh(g), cos, sin, ret_S, a)
            u3 = sh(u)
            pool = _pool(u3, pool_buf[a], p['w_pool'][a], p['pool_scale'][a], pos0)
            acts, w_out, lo = [ret.reshape(B * L, rw), pool.reshape(B * L, PW)], p['w_out_a'], a
            o_S.append(S)
            o_pool.append(u3[:, L - pool_buf.shape[2]:])
        else:
            c = l // 2
            qk, v, og, gt = _proj(x2, p['norm_mix'][l], p['w_in_c'], c, mv, [F32, BF16, BF16], small_cols=LANES)
            gate_bias = jnp.pad(jnp.concatenate([p['b_i'][c], p['b_f'][c]]), (0, LANES - 2 * MH)).reshape(1, LANES)
            qk3 = sh(qk)
            y, C, n, m = _mlstm(qk3, sh(v), sh(og), sh(gt), ml_conv[c], p['conv_c_w'][c], p['conv_c_b'][c],
                                gate_bias, p['ml_norm'][c], ml_C, c, ml_n[c], ml_m[c])
            acts, w_out, lo = [y.reshape(B * L, mv)], p['w_out_c'], c
            o_C.append(C)
            o_n.append(n)
            o_m.append(m)
            o_conv.append(qk3[:, L - ml_conv.shape[2]:])
        x, fb = _ffn(acts, w_out, lo, x, p['norm_ffn'][l], p['w_up'], p['w_gate'], p['ffn_conv_w'],
                     p['ffn_conv_b'], p['w_down'], l, ffn_buf[l])
        o_ffn.append(fb)
    y = _final_norm(x.reshape(B * L, D), p['norm_final']).reshape(B, L, D)
    return (y, jnp.stack(o_S), jnp.stack(o_pool), jnp.stack(o_C), jnp.stack(o_n), jnp.stack(o_m),
            jnp.stack(o_conv), jnp.stack(o_ffn))


def kernel(x_prompt, x_sample, state_ret, cache_pool, state_mlstm_C, state_mlstm_n, state_mlstm_m, cache_mlstm_conv, cache_ffn_conv, norm_mix, norm_ffn, norm_final, w_in_a, w_pool, pool_scale, w_out_a, w_in_c, conv_c_w, conv_c_b, b_i, b_f, ml_norm, w_out_c, w_up, w_gate, ffn_conv_w, ffn_conv_b, w_down):
    bf = lambda w: w.astype(BF16)
    gate_pad = -w_in_c.shape[2] % LANES
    w_in_c_p = jnp.pad(bf(w_in_c), ((0, 0), (0, 0), (0, gate_pad)))
    p = dict(norm_mix=norm_mix, norm_ffn=norm_ffn, norm_final=norm_final, w_in_a=bf(w_in_a), w_pool=bf(w_pool),
             pool_scale=pool_scale, w_out_a=bf(w_out_a), w_in_c=w_in_c_p, conv_c_w=conv_c_w, conv_c_b=conv_c_b,
             b_i=b_i, b_f=b_f, ml_norm=ml_norm, w_out_c=bf(w_out_c), w_up=bf(w_up), w_gate=bf(w_gate),
             ffn_conv_w=ffn_conv_w, ffn_conv_b=ffn_conv_b, w_down=bf(w_down))
    B0 = x_prompt.shape[0]
    zeros = lambda a: jnp.zeros((a.shape[0], B0) + a.shape[2:], F32)
    prompt = _trunk(x_prompt, 0, zeros(state_ret), zeros(cache_pool), zeros(state_mlstm_C), zeros(state_mlstm_n),
                    zeros(state_mlstm_m), zeros(cache_mlstm_conv), zeros(cache_ffn_conv), p)
    sample = _trunk(x_sample, PAST_LEN, state_ret, cache_pool, state_mlstm_C, state_mlstm_n, state_mlstm_m,
                    cache_mlstm_conv, cache_ffn_conv, p)
    return (prompt[0], sample[0]) + tuple(prompt[1:]) + tuple(sample[1:])
```

```python
import functools
import math

import jax
import jax.numpy as jnp
from jax import lax
from jax.experimental import pallas as pl
from jax.experimental.pallas import tpu as pltpu

EPS = 1e-6
PAST_LEN = 2048
POOL_WINDOWS = (2, 4, 8, 16)
F32 = jnp.float32
BF16 = jnp.bfloat16

SUBLANES = 8
LANES = 128
POOL_HALO = 16
VMEM_LIMIT = 56 * 1024 * 1024

ROW_TILE = 512
PROJ_ROWS = 1024
PROJ_COLS = 256
OUT_COLS = 512
FFN_COLS = 512
FFN_PARTS = 2
FFN_CHUNK_VREGS = 32
RET_CHUNK = 256
ML_CHUNK = 256
POOL_ROWS = 512


def _params(n_axes):
    return pltpu.CompilerParams(dimension_semantics=("arbitrary",) * n_axes, vmem_limit_bytes=VMEM_LIMIT)


def _rms(x, g):
    return x * lax.rsqrt(jnp.mean(x * x, -1, keepdims=True) + EPS) * g


def _dot(a, b):
    return jnp.dot(a, b, preferred_element_type=F32)


def _dot_nt(a, b):
    return lax.dot_general(a, b, (((1,), (1,)), ((), ())), preferred_element_type=F32)


def _dot_tn(a, b):
    return lax.dot_general(a, b, (((0,), (0,)), ((), ())), preferred_element_type=F32)


def _causal_dwconv(win, w_ref, cols, n):
    W = w_ref.shape[0]
    acc = None
    for t in range(W):
        d = W - 1 - t
        rows = (pltpu.roll(win, d, 0) if d else win)[SUBLANES:SUBLANES + n]
        term = rows * w_ref[t:t + 1, cols]
        acc = term if acc is None else acc + term
    return acc


def _proj_kernel(*refs, n_seg, has_small):
    x_ref, g_ref = refs[0], refs[1]
    k = 2
    w_refs = refs[k:k + n_seg]
    k += n_seg
    ws_ref = refs[k] if has_small else None
    k += has_small
    o_refs = refs[k:k + n_seg]
    k += n_seg
    os_ref = refs[k] if has_small else None
    k += has_small
    h_ref = refs[k]

    @pl.when(pl.program_id(1) == 0)
    def _():
        h_ref[...] = _rms(x_ref[...], g_ref[...]).astype(BF16)
        if has_small:
            os_ref[...] = _dot(h_ref[...], ws_ref[...])

    h = h_ref[...]
    for w_ref, o_ref in zip(w_refs, o_refs):
        o_ref[...] = _dot(h, w_ref[...]).astype(o_ref.dtype)


def _proj(x, g, w, layer, wseg, out_dtypes, small_cols=0):
    M, D = x.shape
    n_seg = len(out_dtypes)
    tm, tn = min(PROJ_ROWS, M), min(PROJ_COLS, wseg)
    assert M % tm == 0 and wseg % tn == 0 and w.shape[2] >= n_seg * wseg + small_cols
    has_small = small_cols > 0
    nt = wseg // tn
    in_specs = [pl.BlockSpec((tm, D), lambda i, j: (i, 0)), pl.BlockSpec((1, D), lambda i, j: (0, 0))]
    in_specs += [pl.BlockSpec((None, D, tn), functools.partial(lambda i, j, s: (layer, 0, s * nt + j), s=s))
                 for s in range(n_seg)]
    out_specs = [pl.BlockSpec((tm, tn), lambda i, j: (i, j)) for _ in range(n_seg)]
    out_shape = [jax.ShapeDtypeStruct((M, wseg), dt) for dt in out_dtypes]
    args = [x, g.reshape(1, D)] + [w] * n_seg
    if has_small:
        assert (n_seg * wseg) % small_cols == 0
        in_specs.append(pl.BlockSpec((None, D, small_cols), lambda i, j: (layer, 0, n_seg * wseg // small_cols)))
        out_specs.append(pl.BlockSpec((tm, small_cols), lambda i, j: (i, 0)))
        out_shape.append(jax.ShapeDtypeStruct((M, small_cols), F32))
        args.append(w)
    return pl.pallas_call(
        functools.partial(_proj_kernel, n_seg=n_seg, has_small=has_small),
        grid=(M // tm, nt),
        in_specs=in_specs, out_specs=out_specs, out_shape=out_shape,
        scratch_shapes=[pltpu.VMEM((tm, D), BF16)],
        compiler_params=_params(2), name="norm_proj",
    )(*args)


def _ffn_kernel(*refs, n_act, S, R, W, nj, cr, tn):
    a_refs = refs[:n_act]
    (wo_ref, x_ref, g_ref, wu_ref, wg_ref, cw_ref, cb_ref, wd_ref, cache_ref, o_ref, nc_ref,
     h_ref, ext_ref, up_ref, act_ref, carry_ref) = refs[n_act:]
    r, j = pl.program_id(1), pl.program_id(2)
    D = x_ref.shape[-1]
    tf = wu_ref.shape[1]
    H8 = SUBLANES

    @pl.when(j == 0)
    def _():
        @pl.when(r == 0)
        def _():
            for jj in range(nj):
                carry_ref[jj] = cache_ref[:, :, jj * tf:(jj + 1) * tf]

        for n0 in range(0, D, tn):
            acc = x_ref[:, :, n0:n0 + tn].reshape(S * R, tn)
            off = 0
            for a_ref in a_refs:
                kk = a_ref.shape[1]
                acc = acc + _dot(a_ref[...], wo_ref[off:off + kk, n0:n0 + tn])
                off += kk
            o_ref[:, :, n0:n0 + tn] = acc.reshape(S, R, tn)
        h_ref[...] = _rms(o_ref[...].reshape(S * R, D), g_ref[...]).astype(BF16)

    P = FFN_PARTS
    half = S * R // P
    hs = S // P if S > 1 else 1

    def up_dot(k):
        rows = slice(k * half, (k + 1) * half)
        up_ref[rows, :] = _dot(h_ref[rows, :], wu_ref[...])

    def gate_dot(k):
        rows = slice(k * half, (k + 1) * half)
        g2 = _dot(h_ref[rows, :], wg_ref[...])
        if S > 1:
            ext_ref[k * hs:(k + 1) * hs, H8:H8 + R, :] = g2.reshape(hs, R, tf)
        else:
            ext_ref[0, H8 + k * half:H8 + (k + 1) * half, :] = g2

    def chunk(c):
        s, r0 = divmod(c * cr, R)
        acc = _causal_dwconv(ext_ref[s, r0:r0 + H8 + cr, :], cw_ref, slice(None), cr)
        rows = slice(s * R + r0, s * R + r0 + cr)
        act_ref[rows, :] = (jax.nn.gelu(acc + cb_ref[...]) * up_ref[rows, :]).astype(BF16)

    def down_dot(k):
        y = _dot(act_ref[k * half:(k + 1) * half, :], wd_ref[...])
        if S > 1:
            o_ref[k * hs:(k + 1) * hs] += y.reshape(hs, R, D)
        else:
            o_ref[0, k * half:(k + 1) * half, :] += y

    nch = half // cr

    def chunks(k, lo, hi):
        for c in range(k * nch + lo, k * nch + hi):
            chunk(c)

    ext_ref[:, 0:H8, :] = carry_ref[j]
    up_dot(0)
    gate_dot(0)
    for k in range(1, P):
        up_dot(k)
        chunks(k - 1, 0, nch // 2)
        gate_dot(k)
        chunks(k - 1, nch // 2, nch)
        if k >= 2:
            down_dot(k - 2)
    carry_ref[j] = ext_ref[:, R:R + H8, :]
    nc_ref[:, 0] = ext_ref[:, H8 + R - (W - 1):H8 + R, :]
    if P >= 2:
        down_dot(P - 2)
    chunks(P - 1, 0, nch)
    down_dot(P - 1)


def _ffn(acts, w_out, lo, x, g, w_up, w_gate, conv_w, conv_b, w_down, lf, cache):
    B, L, D = x.shape
    F = w_up.shape[2]
    W = conv_w.shape[1]
    K = w_out.shape[1]
    R = min(ROW_TILE, L)
    S = min(B, max(1, ROW_TILE // R))
    tf, tn = min(FFN_COLS, F), min(OUT_COLS, D)
    assert L % R == 0 and B % S == 0 and F % tf == 0 and D % tn == 0 and R % SUBLANES == 0 and R >= SUBLANES >= W - 1
    assert sum(a.shape[1] for a in acts) == K and (S == 1 or R == L)
    nj = F // tf
    cr = min(R, FFN_CHUNK_VREGS * SUBLANES * LANES // tf)
    assert R % cr == 0 and cr % (2 * SUBLANES) == 0 and (S * R // FFN_PARTS) % (2 * cr) == 0
    assert S == 1 or S % FFN_PARTS == 0
    cache8 = jnp.pad(cache, ((0, 0), (SUBLANES - (W - 1), 0), (0, 0)))
    nr = L // R
    in_specs = [pl.BlockSpec((S * R, a.shape[1]), lambda b, r, j: (b * nr + r, 0)) for a in acts]
    in_specs += [
        pl.BlockSpec((None, K, D), lambda b, r, j: (lo, 0, 0), pipeline_mode=pl.Buffered(1)),
        pl.BlockSpec((S, R, D), lambda b, r, j: (b, r, 0)),
        pl.BlockSpec((1, D), lambda b, r, j: (0, 0)),
        pl.BlockSpec((None, D, tf), lambda b, r, j: (lf, 0, j)),
        pl.BlockSpec((None, D, tf), lambda b, r, j: (lf, 0, j)),
        pl.BlockSpec((None, W, tf), lambda b, r, j: (lf, 0, j)),
        pl.BlockSpec((None, 1, tf), lambda b, r, j: (lf, 0, j)),
        pl.BlockSpec((None, tf, D), lambda b, r, j: (lf, j, 0)),
        pl.BlockSpec((S, SUBLANES, F), lambda b, r, j: (b, 0, 0)),
    ]
    y, tails = pl.pallas_call(
        functools.partial(_ffn_kernel, n_act=len(acts), S=S, R=R, W=W, nj=nj, cr=cr, tn=tn),
        grid=(B // S, nr, nj),
        in_specs=in_specs,
        out_specs=[
            pl.BlockSpec((S, R, D), lambda b, r, j: (b, r, 0)),
            pl.BlockSpec((S, 1, W - 1, tf), lambda b, r, j: (b, r, 0, j)),
        ],
        out_shape=[jax.ShapeDtypeStruct((B, L, D), F32), jax.ShapeDtypeStruct((B, nr, W - 1, F), F32)],
        scratch_shapes=[
            pltpu.VMEM((S * R, D), BF16),
            pltpu.VMEM((S, R + SUBLANES, tf), F32),
            pltpu.VMEM((S * R, tf), F32),
            pltpu.VMEM((S * R, tf), BF16),
            pltpu.VMEM((nj, S, SUBLANES, tf), F32),
        ],
        compiler_params=_params(3), name="conv_ffn",
    )(*acts, w_out, x, g.reshape(1, D), w_up, w_gate, conv_w, conv_b.reshape(-1, 1, F), w_down, cache8)
    return y, tails[:, -1]


def _ret_kernel(q_ref, k_ref, v_ref, g_ref, cos_ref, sin_ref, s0_ref, o_ref, so_ref, S_ref, dec_ref, *, H, dk, dv, Lc):
    c = pl.program_id(1)
    log_gamma = lambda h: math.log1p(-(2.0 ** (-5.0 - h)))

    @pl.when(jnp.logical_and(pl.program_id(0) == 0, c == 0))
    def _():
        diff = (lax.broadcasted_iota(jnp.int32, (Lc, Lc), 0) - lax.broadcasted_iota(jnp.int32, (Lc, Lc), 1)).astype(F32)
        for h in range(H):
            dec_ref[h] = jnp.where(diff >= 0, jnp.exp(log_gamma(h) * jnp.maximum(diff, 0.0)), 0.0)

    @pl.when(c == 0)
    def _():
        S_ref[...] = s0_ref[0]

    idx = lax.broadcasted_iota(jnp.int32, (Lc, 1), 0).astype(F32)
    cos, sin = cos_ref[...], sin_ref[...]

    def stage1(h):
        q = q_ref[0, :, h * dk:(h + 1) * dk].astype(F32)
        k = k_ref[0, :, h * dk:(h + 1) * dk].astype(F32)
        qr = q * cos + pltpu.roll(q, dk // 2, 1) * sin
        kr = (k * cos + pltpu.roll(k, dk // 2, 1) * sin) * (dk ** -0.5)
        qb = qr.astype(BF16)
        s = _dot_nt(qb, kr.astype(BF16)) * dec_ref[h]
        return qb, kr, s

    def stage2(h, t):
        lg = log_gamma(h)
        qb, kr, s = t
        v = v_ref[0, :, h * dv:(h + 1) * dv]
        Sh = S_ref[h]
        o = _dot(s.astype(BF16), v) + _dot(qb, Sh.astype(BF16)) * jnp.exp((idx + 1.0) * lg)
        kd = kr * jnp.exp((Lc - 1.0 - idx) * lg)
        S_ref[h] = math.exp(Lc * lg) * Sh + _dot_tn(kd.astype(BF16), v)
        return o

    def stage3(h, o):
        on = o * lax.rsqrt(jnp.mean(o * o, -1, keepdims=True) + EPS)
        gg = g_ref[0, :, h * dv:(h + 1) * dv].astype(F32)
        o_ref[0, :, h * dv:(h + 1) * dv] = (on * (gg * jax.nn.sigmoid(gg))).astype(o_ref.dtype)

    t1, t2 = {}, {}
    for step in range(H + 2):
        if step < H:
            t1[step] = stage1(step)
        if 0 <= step - 1 < H:
            t2[step - 1] = stage2(step - 1, t1.pop(step - 1))
        if 0 <= step - 2 < H:
            stage3(step - 2, t2.pop(step - 2))

    @pl.when(c == pl.num_programs(1) - 1)
    def _():
        so_ref[0] = S_ref[...]


def _retention(q, k, v, g, cos, sin, S0, layer):
    B, L, _ = q.shape
    _, _, H, dk, dv = S0.shape
    Lc = min(RET_CHUNK, L)
    assert L % Lc == 0
    seq = lambda w: pl.BlockSpec((1, Lc, w), lambda b, c: (b, c, 0))
    tab = pl.BlockSpec((Lc, dk), lambda b, c: (c, 0))
    return pl.pallas_call(
        functools.partial(_ret_kernel, H=H, dk=dk, dv=dv, Lc=Lc),
        grid=(B, L // Lc),
        in_specs=[seq(H * dk), seq(H * dk), seq(H * dv), seq(H * dv), tab, tab,
                  pl.BlockSpec((None, 1, H, dk, dv), lambda b, c: (layer, b, 0, 0, 0))],
        out_specs=[seq(H * dv), pl.BlockSpec((1, H, dk, dv), lambda b, c: (b, 0, 0, 0))],
        out_shape=[jax.ShapeDtypeStruct((B, L, H * dv), BF16), jax.ShapeDtypeStruct(S0.shape[1:], F32)],
        scratch_shapes=[pltpu.VMEM((H, dk, dv), F32), pltpu.VMEM((H, Lc, Lc), F32)],
        compiler_params=_params(2), name="retention",
    )(q, k, v, g, cos, sin, S0)


def _pool_kernel(u_ref, cache_ref, wp_ref, ps_ref, o_ref, ext_ref, *, tr, gd, pos0):
    r = pl.program_id(1)
    PB = POOL_HALO

    @pl.when(r == 0)
    def _():
        ext_ref[0:PB, :] = cache_ref[0]

    @pl.when(r > 0)
    def _():
        ext_ref[0:PB, :] = ext_ref[tr:tr + PB, :]

    ext_ref[PB:PB + tr, :] = u_ref[0]
    pos = pos0 + r * tr + lax.broadcasted_iota(jnp.int32, (tr, 1), 0)
    for j, w in enumerate(POOL_WINDOWS):
        lo, hi = j * gd, (j + 1) * gd
        acc = ext_ref[:, lo:hi]
        sh = 1
        while sh < w:
            acc = acc + pltpu.roll(acc, sh, 0)
            sh *= 2
        acc = acc[PB:PB + tr]
        cnt = jnp.minimum(pos + 1, w).astype(F32)
        pooled = acc / cnt - u_ref[0, :, lo:hi]
        y = _dot(pooled.astype(BF16), wp_ref[j]) * ps_ref[:, lo:hi]
        o_ref[0, :, lo:hi] = y.astype(o_ref.dtype)


def _pool(u, cache, w_pool, pool_scale, pos0):
    B, L, P = u.shape
    G, gd, _ = w_pool.shape
    assert G == len(POOL_WINDOWS) and G * gd == P and cache.shape[1] == max(POOL_WINDOWS) - 1
    tr = min(POOL_ROWS, L)
    assert L % tr == 0 and tr >= POOL_HALO
    cache_h = jnp.pad(cache, ((0, 0), (POOL_HALO - cache.shape[1], 0), (0, 0)))
    return pl.pallas_call(
        functools.partial(_pool_kernel, tr=tr, gd=gd, pos0=pos0),
        grid=(B, L // tr),
        in_specs=[
            pl.BlockSpec((1, tr, P), lambda b, r: (b, r, 0)),
            pl.BlockSpec((1, POOL_HALO, P), lambda b, r: (b, 0, 0)),
            pl.BlockSpec((G, gd, gd), lambda b, r: (0, 0, 0)),
            pl.BlockSpec((1, P), lambda b, r: (0, 0)),
        ],
        out_specs=pl.BlockSpec((1, tr, P), lambda b, r: (b, r, 0)),
        out_shape=jax.ShapeDtypeStruct((B, L, P), BF16),
        scratch_shapes=[pltpu.VMEM((tr + POOL_HALO, P), F32)],
        compiler_params=_params(2), name="ms_pool",
    )(u, cache_h, w_pool, pool_scale.reshape(1, P))


def _mlstm_kernel(qk_ref, v_ref, og_ref, gt_ref, cache_ref, cw_ref, cb_ref, bias_ref, ng_ref,
                  C0_ref, n0_ref, m0_ref, y_ref, Co_ref, no_ref, mo_ref,
                  ext_ref, C_ref, n_ref, m_ref, *, H, dk, dv, Lc, W):
    c = pl.program_id(1)
    H8 = SUBLANES

    @pl.when(c == 0)
    def _():
        ext_ref[0:H8, :] = cache_ref[0]
        C_ref[...] = C0_ref[0]
        n_ref[...] = n0_ref[0]
        m_ref[...] = m0_ref[0]

    @pl.when(c > 0)
    def _():
        ext_ref[0:H8, :] = ext_ref[Lc:Lc + H8, :]

    ext_ref[H8:H8 + Lc, :] = qk_ref[0]

    def conv_silu(lo, hi):
        pre = _causal_dwconv(ext_ref[:, lo:hi], cw_ref, slice(lo, hi), Lc) + cb_ref[:, lo:hi]
        return pre * jax.nn.sigmoid(pre)

    gt = gt_ref[0] + bias_ref[...]
    lane = lax.broadcasted_iota(jnp.int32, gt.shape, 1)
    row = lax.broadcasted_iota(jnp.int32, gt.shape, 0)
    b = jax.nn.log_sigmoid(gt)
    sh = 1
    while sh < Lc:
        b = b + jnp.where(row >= sh, pltpu.roll(b, sh, 0), 0.0)
        sh *= 2
    A = jnp.where(lane < H, gt, b)
    AT = A.T
    causal = lax.broadcasted_iota(jnp.int32, (Lc, Lc), 0) >= lax.broadcasted_iota(jnp.int32, (Lc, Lc), 1)

    def stage1(h):
        i_col, b_col = A[:, h:h + 1], A[:, H + h:H + h + 1]
        i_row, b_row = AT[h:h + 1, :], AT[H + h:H + h + 1, :]
        m_prev = m_ref[h:h + 1, 0:1]
        dlog = jnp.where(causal, b_col - b_row + i_row, -jnp.inf)
        inter_log = b_col + m_prev
        m_t = jnp.maximum(inter_log, jnp.max(dlog, -1, keepdims=True))
        dw = jnp.exp(dlog - m_t)
        iw = jnp.exp(inter_log - m_t)
        q = conv_silu(h * dk, (h + 1) * dk)
        k = conv_silu((H + h) * dk, (H + h + 1) * dk) * (dk ** -0.5)
        qb = q.astype(BF16)
        s = _dot_nt(qb, k.astype(BF16)) * dw
        return dict(i_col=i_col, b_col=b_col, m_prev=m_prev, m_t=m_t, iw=iw, q=q, k=k, qb=qb, s=s)

    def stage2(h, t):
        v = v_ref[0, :, h * dv:(h + 1) * dv]
        Ch = C_ref[h]
        nh = n_ref[h:h + 1, :]
        s, iw, m_t, b_col = t['s'], t['iw'], t['m_t'], t['b_col']
        num = _dot(s.astype(BF16), v) + _dot(t['qb'], Ch.astype(BF16)) * iw
        den = jnp.sum(s, -1, keepdims=True) + jnp.sum(t['q'] * nh, -1, keepdims=True) * iw
        hh = num / jnp.maximum(jnp.abs(den), jnp.exp(-m_t))
        m_L, b_L = m_t[Lc - 1:Lc, :], b_col[Lc - 1:Lc, :]
        kw = t['k'] * jnp.exp(b_L - b_col + t['i_col'] - m_L)
        decay = jnp.exp(b_L + t['m_prev'] - m_L)
        C_ref[h] = decay * Ch + _dot_tn(kw.astype(BF16), v)
        n_ref[h:h + 1, :] = decay * nh + jnp.sum(kw, 0, keepdims=True)
        m_ref[h:h + 1, :] = jnp.broadcast_to(m_L, (1, m_ref.shape[1]))
        return hh

    def stage3(h, hh):
        hn = hh * lax.rsqrt(jnp.mean(hh * hh, -1, keepdims=True) + EPS) * ng_ref[:, h * dv:(h + 1) * dv]
        og = og_ref[0, :, h * dv:(h + 1) * dv].astype(F32)
        y_ref[0, :, h * dv:(h + 1) * dv] = (hn * jax.nn.sigmoid(og)).astype(y_ref.dtype)

    t1, t2 = {}, {}
    for step in range(H + 2):
        if step < H:
            t1[step] = stage1(step)
        if 0 <= step - 1 < H:
            t2[step - 1] = stage2(step - 1, t1.pop(step - 1))
        if 0 <= step - 2 < H:
            stage3(step - 2, t2.pop(step - 2))

    @pl.when(c == pl.num_programs(1) - 1)
    def _():
        Co_ref[0] = C_ref[...]
        no_ref[0] = n_ref[...]
        mo_ref[0] = m_ref[...]


def _mlstm(qk, v, og, gt, cache, conv_w, conv_b, gate_bias, norm_g, C0, layer, n0, m0):
    B, L, QK = qk.shape
    _, _, H, dk, dv = C0.shape
    W = conv_w.shape[0]
    NG = gt.shape[-1]
    Lc = min(ML_CHUNK, L)
    assert L % Lc == 0 and Lc >= SUBLANES >= W - 1 and QK == 2 * H * dk and H == SUBLANES and dk == LANES
    cache8 = jnp.pad(cache, ((0, 0), (SUBLANES - (W - 1), 0), (0, 0)))
    m0b = jnp.broadcast_to(m0[:, :, None], (B, H, LANES))
    seq = lambda w: pl.BlockSpec((1, Lc, w), lambda b, c: (b, c, 0))
    const = lambda shape: pl.BlockSpec(shape, lambda b, c: (0,) * len(shape))
    per_b = lambda shape: pl.BlockSpec((1,) + shape, lambda b, c: (b,) + (0,) * len(shape))
    y, C, n, m = pl.pallas_call(
        functools.partial(_mlstm_kernel, H=H, dk=dk, dv=dv, Lc=Lc, W=W),
        grid=(B, L // Lc),
        in_specs=[seq(QK), seq(H * dv), seq(H * dv), seq(NG), per_b((SUBLANES, QK)),
                  const((W, QK)), const((1, QK)), const((1, NG)), const((1, H * dv)),
                  pl.BlockSpec((None, 1, H, dk, dv), lambda b, c: (layer, b, 0, 0, 0)),
                  per_b((H, dk)), per_b((H, LANES))],
        out_specs=[seq(H * dv), per_b((H, dk, dv)), per_b((H, dk)), per_b((H, LANES))],
        out_shape=[jax.ShapeDtypeStruct((B, L, H * dv), BF16), jax.ShapeDtypeStruct(C0.shape[1:], F32),
                   jax.ShapeDtypeStruct(n0.shape, F32), jax.ShapeDtypeStruct((B, H, LANES), F32)],
        scratch_shapes=[pltpu.VMEM((Lc + SUBLANES, QK), F32), pltpu.VMEM((H, dk, dv), F32),
                        pltpu.VMEM((H, dk), F32), pltpu.VMEM((H, LANES), F32)],
        compiler_params=_params(2), name="mlstm",
    )(qk, v, og, gt, cache8, conv_w, conv_b.reshape(1, QK), gate_bias, norm_g.reshape(1, H * dv), C0, n0, m0b)
    return y, C, n, m[:, :, 0]


def _norm_kernel(x_ref, g_ref, o_ref):
    o_ref[...] = _rms(x_ref[...], g_ref[...])


def _final_norm(x, g):
    M, D = x.shape
    tm = min(ROW_TILE, M)
    assert M % tm == 0
    return pl.pallas_call(
        _norm_kernel, grid=(M // tm,),
        in_specs=[pl.BlockSpec((tm, D), lambda i: (i, 0)), pl.BlockSpec((1, D), lambda i: (0, 0))],
        out_specs=pl.BlockSpec((tm, D), lambda i: (i, 0)),
        out_shape=jax.ShapeDtypeStruct((M, D), F32),
        compiler_params=_params(1), name="final_norm",
    )(x, g.reshape(1, D))


def _rotary_tables(pos, dk):
    half = dk // 2
    inv = 1.0 / (10000.0 ** jnp.linspace(0.0, 1.0, half, dtype=F32))
    ang = pos.astype(F32)[:, None] * inv[None, :]
    cos, sin = jnp.cos(ang), jnp.sin(ang)
    return jnp.concatenate([cos, cos], -1), jnp.concatenate([-sin, sin], -1)


def _trunk(x, pos0, ret_S, pool_buf, ml_C, ml_n, ml_m, ml_conv, ffn_buf, p):
    B, L, D = x.shape
    depth = p['norm_mix'].shape[0]
    _, _, RH, rdk, rdv = ret_S.shape
    _, _, MH, mdk, mdv = ml_C.shape
    PW = pool_buf.shape[-1]
    qw, rw, mqk, mv = RH * rdk, RH * rdv, MH * mdk, MH * mdv
    assert L >= pool_buf.shape[2] and L >= ml_conv.shape[2]
    assert qw == rw == PW and p['w_in_a'].shape[2] == 2 * qw + 2 * rw + PW
    assert mv == mqk * 2 and p['w_in_c'].shape[2] >= 2 * mqk + 2 * mv + LANES and 2 * MH <= LANES
    cos, sin = _rotary_tables(pos0 + jnp.arange(L, dtype=jnp.int32), rdk)
    sh = lambda t: t.reshape(B, L, -1)
    o_S, o_pool, o_C, o_n, o_m, o_conv, o_ffn = [], [], [], [], [], [], []
    for l in range(depth):
        x2 = x.reshape(B * L, D)
        if l % 2 == 0:
            a = l // 2
            q, k, v, g, u = _proj(x2, p['norm_mix'][l], p['w_in_a'], a, qw, [BF16, BF16, BF16, BF16, F32])
            ret, S = _retention(sh(q), sh(k), sh(v), sh(g), cos, sin, ret_S, a)
            u3 = sh(u)
            pool = _pool(u3, pool_buf[a], p['w_pool'][a], p['pool_scale'][a], pos0)
            acts, w_out, lo = [ret.reshape(B * L, rw), pool.reshape(B * L, PW)], p['w_out_a'], a
            o_S.append(S)
            o_pool.append(u3[:, L - pool_buf.shape[2]:])
        else:
            c = l // 2
            qk, v, og, gt = _proj(x2, p['norm_mix'][l], p['w_in_c'], c, mv, [F32, BF16, BF16], small_cols=LANES)
            gate_bias = jnp.pad(jnp.concatenate([p['b_i'][c], p['b_f'][c]]), (0, LANES - 2 * MH)).reshape(1, LANES)
            qk3 = sh(qk)
            y, C, n, m = _mlstm(qk3, sh(v), sh(og), sh(gt), ml_conv[c], p['conv_c_w'][c], p['conv_c_b'][c],
                                gate_bias, p['ml_norm'][c], ml_C, c, ml_n[c], ml_m[c])
            acts, w_out, lo = [y.reshape(B * L, mv)], p['w_out_c'], c
            o_C.append(C)
            o_n.append(n)
            o_m.append(m)
            o_conv.append(qk3[:, L - ml_conv.shape[2]:])
        x, fb = _ffn(acts, w_out, lo, x, p['norm_ffn'][l], p['w_up'], p['w_gate'], p['ffn_conv_w'],
                     p['ffn_conv_b'], p['w_down'], l, ffn_buf[l])
        o_ffn.append(fb)
    y = _final_norm(x.reshape(B * L, D), p['norm_final']).reshape(B, L, D)
    return (y, jnp.stack(o_S), jnp.stack(o_pool), jnp.stack(o_C), jnp.stack(o_n), jnp.stack(o_m),
            jnp.stack(o_conv), jnp.stack(o_ffn))


def kernel(x_prompt, x_sample, state_ret, cache_pool, state_mlstm_C, state_mlstm_n, state_mlstm_m, cache_mlstm_conv, cache_ffn_conv, norm_mix, norm_ffn, norm_final, w_in_a, w_pool, pool_scale, w_out_a, w_in_c, conv_c_w, conv_c_b, b_i, b_f, ml_norm, w_out_c, w_up, w_gate, ffn_conv_w, ffn_conv_b, w_down):
    bf = lambda w: w.astype(BF16)
    gate_pad = -w_in_c.shape[2] % LANES
    w_in_c_p = jnp.pad(bf(w_in_c), ((0, 0), (0, 0), (0, gate_pad)))
    p = dict(norm_mix=norm_mix, norm_ffn=norm_ffn, norm_final=norm_final, w_in_a=bf(w_in_a), w_pool=bf(w_pool),
             pool_scale=pool_scale, w_out_a=bf(w_out_a), w_in_c=w_in_c_p, conv_c_w=conv_c_w, conv_c_b=conv_c_b,
             b_i=b_i, b_f=b_f, ml_norm=ml_norm, w_out_c=bf(w_out_c), w_up=bf(w_up), w_gate=bf(w_gate),
             ffn_conv_w=ffn_conv_w, ffn_conv_b=ffn_conv_b, w_down=bf(w_down))
    B0 = x_prompt.shape[0]
    zeros = lambda a: jnp.zeros((a.shape[0], B0) + a.shape[2:], F32)
    prompt = _trunk(x_prompt, 0, zeros(state_ret), zeros(cache_pool), zeros(state_mlstm_C), zeros(state_mlstm_n),
                    zeros(state_mlstm_m), zeros(cache_mlstm_conv), zeros(cache_ffn_conv), p)
    sample = _trunk(x_sample, PAST_LEN, state_ret, cache_pool, state_mlstm_C, state_mlstm_n, state_mlstm_m,
                    cache_mlstm_conv, cache_ffn_conv, p)
    return (prompt[0], sample[0]) + tuple(prompt[1:]) + tuple(sample[1:])
```

```python
import functools
import math

import jax
import jax.numpy as jnp
from jax import lax
from jax.experimental import pallas as pl
from jax.experimental.pallas import tpu as pltpu

EPS = 1e-6
PAST_LEN = 2048
POOL_WINDOWS = (2, 4, 8, 16)
F32 = jnp.float32
BF16 = jnp.bfloat16

SUBLANES = 8
LANES = 128
POOL_HALO = 16
VMEM_LIMIT = 56 * 1024 * 1024

ROW_TILE = 512
PROJ_ROWS = 1024
PROJ_COLS = 256
OUT_COLS = 512
FFN_COLS = 512
FFN_PARTS = 2
FFN_CHUNK_VREGS = 32
RET_CHUNK = 256
ML_CHUNK = 256
POOL_ROWS = 512


def _params(n_axes):
    return pltpu.CompilerParams(dimension_semantics=("arbitrary",) * n_axes, vmem_limit_bytes=VMEM_LIMIT)


def _rms(x, g):
    return x * lax.rsqrt(jnp.mean(x * x, -1, keepdims=True) + EPS) * g


def _dot(a, b):
    return jnp.dot(a, b, preferred_element_type=F32)


def _dot_nt(a, b):
    return lax.dot_general(a, b, (((1,), (1,)), ((), ())), preferred_element_type=F32)


def _dot_tn(a, b):
    return lax.dot_general(a, b, (((0,), (0,)), ((), ())), preferred_element_type=F32)


def _causal_dwconv(win, w_ref, cols, n):
    W = w_ref.shape[0]
    acc = None
    for t in range(W):
        d = W - 1 - t
        rows = (pltpu.roll(win, d, 0) if d else win)[SUBLANES:SUBLANES + n]
        term = rows * w_ref[t:t + 1, cols]
        acc = term if acc is None else acc + term
    return acc


def _proj_kernel(*refs, n_seg, has_small):
    x_ref, g_ref = refs[0], refs[1]
    k = 2
    w_refs = refs[k:k + n_seg]
    k += n_seg
    ws_ref = refs[k] if has_small else None
    k += has_small
    o_refs = refs[k:k + n_seg]
    k += n_seg
    os_ref = refs[k] if has_small else None
    k += has_small
    h_ref = refs[k]

    @pl.when(pl.program_id(1) == 0)
    def _():
        h_ref[...] = _rms(x_ref[...], g_ref[...]).astype(BF16)
        if has_small:
            os_ref[...] = _dot(h_ref[...], ws_ref[...])

    h = h_ref[...]
    for w_ref, o_ref in zip(w_refs, o_refs):
        o_ref[...] = _dot(h, w_ref[...]).astype(o_ref.dtype)


def _proj(x, g, w, layer, wseg, out_dtypes, small_cols=0):
    M, D = x.shape
    n_seg = len(out_dtypes)
    tm, tn = min(PROJ_ROWS, M), min(PROJ_COLS, wseg)
    assert M % tm == 0 and wseg % tn == 0 and w.shape[2] >= n_seg * wseg + small_cols
    has_small = small_cols > 0
    nt = wseg // tn
    in_specs = [pl.BlockSpec((tm, D), lambda i, j: (i, 0)), pl.BlockSpec((1, D), lambda i, j: (0, 0))]
    in_specs += [pl.BlockSpec((None, D, tn), functools.partial(lambda i, j, s: (layer, 0, s * nt + j), s=s))
                 for s in range(n_seg)]
    out_specs = [pl.BlockSpec((tm, tn), lambda i, j: (i, j)) for _ in range(n_seg)]
    out_shape = [jax.ShapeDtypeStruct((M, wseg), dt) for dt in out_dtypes]
    args = [x, g.reshape(1, D)] + [w] * n_seg
    if has_small:
        assert (n_seg * wseg) % small_cols == 0
        in_specs.append(pl.BlockSpec((None, D, small_cols), lambda i, j: (layer, 0, n_seg * wseg // small_cols)))
        out_specs.append(pl.BlockSpec((tm, small_cols), lambda i, j: (i, 0)))
        out_shape.append(jax.ShapeDtypeStruct((M, small_cols), F32))
        args.append(w)
    return pl.pallas_call(
        functools.partial(_proj_kernel, n_seg=n_seg, has_small=has_small),
        grid=(M // tm, nt),
        in_specs=in_specs, out_specs=out_specs, out_shape=out_shape,
        scratch_shapes=[pltpu.VMEM((tm, D), BF16)],
        compiler_params=_params(2), name="norm_proj",
    )(*args)


def _ffn_kernel(*refs, n_act, S, R, W, nj, cr, tn, final):
    a_refs = refs[:n_act]
    (wo_ref, x_ref, g_ref, gf_ref, wu_ref, wg_ref, cw_ref, cb_ref, wd_ref, cache_ref, o_ref, nc_ref,
     h_ref, ext_ref, up_ref, act_ref, carry_ref) = refs[n_act:]
    r, j = pl.program_id(1), pl.program_id(2)
    D = x_ref.shape[-1]
    tf = wu_ref.shape[1]
    H8 = SUBLANES

    @pl.when(j == 0)
    def _():
        @pl.when(r == 0)
        def _():
            for jj in range(nj):
                carry_ref[jj] = cache_ref[:, :, jj * tf:(jj + 1) * tf]

        for n0 in range(0, D, tn):
            acc = x_ref[:, :, n0:n0 + tn].reshape(S * R, tn)
            off = 0
            for a_ref in a_refs:
                kk = a_ref.shape[1]
                acc = acc + _dot(a_ref[...], wo_ref[off:off + kk, n0:n0 + tn])
                off += kk
            o_ref[:, :, n0:n0 + tn] = acc.reshape(S, R, tn)
        h_ref[...] = _rms(o_ref[...].reshape(S * R, D), g_ref[...]).astype(BF16)

    P = FFN_PARTS
    half = S * R // P
    hs = S // P if S > 1 else 1

    def up_dot(k):
        rows = slice(k * half, (k + 1) * half)
        up_ref[rows, :] = _dot(h_ref[rows, :], wu_ref[...])

    def gate_dot(k):
        rows = slice(k * half, (k + 1) * half)
        g2 = _dot(h_ref[rows, :], wg_ref[...])
        if S > 1:
            ext_ref[k * hs:(k + 1) * hs, H8:H8 + R, :] = g2.reshape(hs, R, tf)
        else:
            ext_ref[0, H8 + k * half:H8 + (k + 1) * half, :] = g2

    def chunk(c):
        s, r0 = divmod(c * cr, R)
        acc = _causal_dwconv(ext_ref[s, r0:r0 + H8 + cr, :], cw_ref, slice(None), cr)
        rows = slice(s * R + r0, s * R + r0 + cr)
        act_ref[rows, :] = (jax.nn.gelu(acc + cb_ref[...]) * up_ref[rows, :]).astype(BF16)

    def down_dot(k):
        y = _dot(act_ref[k * half:(k + 1) * half, :], wd_ref[...])
        if S > 1:
            o_ref[k * hs:(k + 1) * hs] += y.reshape(hs, R, D)
        else:
            o_ref[0, k * half:(k + 1) * half, :] += y

    nch = half // cr

    def chunks(k, lo, hi):
        for c in range(k * nch + lo, k * nch + hi):
            chunk(c)

    ext_ref[:, 0:H8, :] = carry_ref[j]
    up_dot(0)
    gate_dot(0)
    for k in range(1, P):
        up_dot(k)
        chunks(k - 1, 0, nch // 2)
        gate_dot(k)
        chunks(k - 1, nch // 2, nch)
        if k >= 2:
            down_dot(k - 2)
    carry_ref[j] = ext_ref[:, R:R + H8, :]
    nc_ref[:, 0] = ext_ref[:, H8 + R - (W - 1):H8 + R, :]
    if P >= 2:
        down_dot(P - 2)
    chunks(P - 1, 0, nch)
    down_dot(P - 1)
    if final:
        @pl.when(j == nj - 1)
        def _():
            o_ref[...] = _rms(o_ref[...].reshape(S * R, D), gf_ref[...]).reshape(S, R, D)


def _ffn(acts, w_out, lo, x, g, w_up, w_gate, conv_w, conv_b, w_down, lf, cache, final_g=None):
    B, L, D = x.shape
    F = w_up.shape[2]
    W = conv_w.shape[1]
    K = w_out.shape[1]
    R = min(ROW_TILE, L)
    S = min(B, max(1, ROW_TILE // R))
    tf, tn = min(FFN_COLS, F), min(OUT_COLS, D)
    assert L % R == 0 and B % S == 0 and F % tf == 0 and D % tn == 0 and R % SUBLANES == 0 and R >= SUBLANES >= W - 1
    assert sum(a.shape[1] for a in acts) == K and (S == 1 or R == L)
    nj = F // tf
    cr = min(R, FFN_CHUNK_VREGS * SUBLANES * LANES // tf)
    assert R % cr == 0 and cr % (2 * SUBLANES) == 0 and (S * R // FFN_PARTS) % (2 * cr) == 0
    assert S == 1 or S % FFN_PARTS == 0
    cache8 = jnp.pad(cache, ((0, 0), (SUBLANES - (W - 1), 0), (0, 0)))
    nr = L // R
    in_specs = [pl.BlockSpec((S * R, a.shape[1]), lambda b, r, j: (b * nr + r, 0)) for a in acts]
    in_specs += [
        pl.BlockSpec((None, K, D), lambda b, r, j: (lo, 0, 0), pipeline_mode=pl.Buffered(1)),
        pl.BlockSpec((S, R, D), lambda b, r, j: (b, r, 0)),
        pl.BlockSpec((1, D), lambda b, r, j: (0, 0)),
        pl.BlockSpec((1, D), lambda b, r, j: (0, 0)),
        pl.BlockSpec((None, D, tf), lambda b, r, j: (lf, 0, j)),
        pl.BlockSpec((None, D, tf), lambda b, r, j: (lf, 0, j)),
        pl.BlockSpec((None, W, tf), lambda b, r, j: (lf, 0, j)),
        pl.BlockSpec((None, 1, tf), lambda b, r, j: (lf, 0, j)),
        pl.BlockSpec((None, tf, D), lambda b, r, j: (lf, j, 0)),
        pl.BlockSpec((S, SUBLANES, F), lambda b, r, j: (b, 0, 0)),
    ]
    y, tails = pl.pallas_call(
        functools.partial(_ffn_kernel, n_act=len(acts), S=S, R=R, W=W, nj=nj, cr=cr, tn=tn, final=final_g is not None),
        grid=(B // S, nr, nj),
        in_specs=in_specs,
        out_specs=[
            pl.BlockSpec((S, R, D), lambda b, r, j: (b, r, 0)),
            pl.BlockSpec((S, 1, W - 1, tf), lambda b, r, j: (b, r, 0, j)),
        ],
        out_shape=[jax.ShapeDtypeStruct((B, L, D), F32), jax.ShapeDtypeStruct((B, nr, W - 1, F), F32)],
        scratch_shapes=[
            pltpu.VMEM((S * R, D), BF16),
            pltpu.VMEM((S, R + SUBLANES, tf), F32),
            pltpu.VMEM((S * R, tf), F32),
            pltpu.VMEM((S * R, tf), BF16),
            pltpu.VMEM((nj, S, SUBLANES, tf), F32),
        ],
        compiler_params=_params(3), name="conv_ffn",
    )(*acts, w_out, x, g.reshape(1, D), (g if final_g is None else final_g).reshape(1, D), w_up, w_gate, conv_w, conv_b.reshape(-1, 1, F), w_down, cache8)
    return y, tails[:, -1]


def _ret_kernel(q_ref, k_ref, v_ref, g_ref, cos_ref, sin_ref, s0_ref, o_ref, so_ref, S_ref, dec_ref, *, H, dk, dv, Lc):
    c = pl.program_id(1)
    log_gamma = lambda h: math.log1p(-(2.0 ** (-5.0 - h)))

    @pl.when(jnp.logical_and(pl.program_id(0) == 0, c == 0))
    def _():
        diff = (lax.broadcasted_iota(jnp.int32, (Lc, Lc), 0) - lax.broadcasted_iota(jnp.int32, (Lc, Lc), 1)).astype(F32)
        for h in range(H):
            dec_ref[h] = jnp.where(diff >= 0, jnp.exp(log_gamma(h) * jnp.maximum(diff, 0.0)), 0.0)

    @pl.when(c == 0)
    def _():
        S_ref[...] = s0_ref[0]

    idx = lax.broadcasted_iota(jnp.int32, (Lc, 1), 0).astype(F32)
    cos, sin = cos_ref[...], sin_ref[...]

    def stage1(h):
        q = q_ref[0, :, h * dk:(h + 1) * dk].astype(F32)
        k = k_ref[0, :, h * dk:(h + 1) * dk].astype(F32)
        qr = q * cos + pltpu.roll(q, dk // 2, 1) * sin
        kr = (k * cos + pltpu.roll(k, dk // 2, 1) * sin) * (dk ** -0.5)
        qb = qr.astype(BF16)
        s = _dot_nt(qb, kr.astype(BF16)) * dec_ref[h]
        return qb, kr, s

    def stage2(h, t):
        lg = log_gamma(h)
        qb, kr, s = t
        v = v_ref[0, :, h * dv:(h + 1) * dv]
        Sh = S_ref[h]
        o = _dot(s.astype(BF16), v) + _dot(qb, Sh.astype(BF16)) * jnp.exp((idx + 1.0) * lg)
        kd = kr * jnp.exp((Lc - 1.0 - idx) * lg)
        S_ref[h] = math.exp(Lc * lg) * Sh + _dot_tn(kd.astype(BF16), v)
        return o

    def stage3(h, o):
        on = o * lax.rsqrt(jnp.mean(o * o, -1, keepdims=True) + EPS)
        gg = g_ref[0, :, h * dv:(h + 1) * dv].astype(F32)
        o_ref[0, :, h * dv:(h + 1) * dv] = (on * (gg * jax.nn.sigmoid(gg))).astype(o_ref.dtype)

    t1, t2 = {}, {}
    for step in range(H + 2):
        if step < H:
            t1[step] = stage1(step)
        if 0 <= step - 1 < H:
            t2[step - 1] = stage2(step - 1, t1.pop(step - 1))
        if 0 <= step - 2 < H:
            stage3(step - 2, t2.pop(step - 2))

    @pl.when(c == pl.num_programs(1) - 1)
    def _():
        so_ref[0] = S_ref[...]


def _retention(q, k, v, g, cos, sin, S0, layer):
    B, L, _ = q.shape
    _, _, H, dk, dv = S0.shape
    Lc = min(RET_CHUNK, L)
    assert L % Lc == 0
    seq = lambda w: pl.BlockSpec((1, Lc, w), lambda b, c: (b, c, 0))
    tab = pl.BlockSpec((Lc, dk), lambda b, c: (c, 0))
    return pl.pallas_call(
        functools.partial(_ret_kernel, H=H, dk=dk, dv=dv, Lc=Lc),
        grid=(B, L // Lc),
        in_specs=[seq(H * dk), seq(H * dk), seq(H * dv), seq(H * dv), tab, tab,
                  pl.BlockSpec((None, 1, H, dk, dv), lambda b, c: (layer, b, 0, 0, 0))],
        out_specs=[seq(H * dv), pl.BlockSpec((1, H, dk, dv), lambda b, c: (b, 0, 0, 0))],
        out_shape=[jax.ShapeDtypeStruct((B, L, H * dv), BF16), jax.ShapeDtypeStruct(S0.shape[1:], F32)],
        scratch_shapes=[pltpu.VMEM((H, dk, dv), F32), pltpu.VMEM((H, Lc, Lc), F32)],
        compiler_params=_params(2), name="retention",
    )(q, k, v, g, cos, sin, S0)


def _pool_kernel(u_ref, cache_ref, wp_ref, ps_ref, o_ref, ext_ref, *, tr, gd, pos0):
    r = pl.program_id(1)
    PB = POOL_HALO

    @pl.when(r == 0)
    def _():
        ext_ref[0:PB, :] = cache_ref[0]

    @pl.when(r > 0)
    def _():
        ext_ref[0:PB, :] = ext_ref[tr:tr + PB, :]

    ext_ref[PB:PB + tr, :] = u_ref[0]
    pos = pos0 + r * tr + lax.broadcasted_iota(jnp.int32, (tr, 1), 0)
    for j, w in enumerate(POOL_WINDOWS):
        lo, hi = j * gd, (j + 1) * gd
        acc = ext_ref[:, lo:hi]
        sh = 1
        while sh < w:
            acc = acc + pltpu.roll(acc, sh, 0)
            sh *= 2
        acc = acc[PB:PB + tr]
        cnt = jnp.minimum(pos + 1, w).astype(F32)
        pooled = acc / cnt - u_ref[0, :, lo:hi]
        y = _dot(pooled.astype(BF16), wp_ref[j]) * ps_ref[:, lo:hi]
        o_ref[0, :, lo:hi] = y.astype(o_ref.dtype)


def _pool(u, cache, w_pool, pool_scale, pos0):
    B, L, P = u.shape
    G, gd, _ = w_pool.shape
    assert G == len(POOL_WINDOWS) and G * gd == P and cache.shape[1] == max(POOL_WINDOWS) - 1
    tr = min(POOL_ROWS, L)
    assert L % tr == 0 and tr >= POOL_HALO
    cache_h = jnp.pad(cache, ((0, 0), (POOL_HALO - cache.shape[1], 0), (0, 0)))
    return pl.pallas_call(
        functools.partial(_pool_kernel, tr=tr, gd=gd, pos0=pos0),
        grid=(B, L // tr),
        in_specs=[
            pl.BlockSpec((1, tr, P), lambda b, r: (b, r, 0)),
            pl.BlockSpec((1, POOL_HALO, P), lambda b, r: (b, 0, 0)),
            pl.BlockSpec((G, gd, gd), lambda b, r: (0, 0, 0)),
            pl.BlockSpec((1, P), lambda b, r: (0, 0)),
        ],
        out_specs=pl.BlockSpec((1, tr, P), lambda b, r: (b, r, 0)),
        out_shape=jax.ShapeDtypeStruct((B, L, P), BF16),
        scratch_shapes=[pltpu.VMEM((tr + POOL_HALO, P), F32)],
        compiler_params=_params(2), name="ms_pool",
    )(u, cache_h, w_pool, pool_scale.reshape(1, P))


def _mlstm_kernel(qk_ref, v_ref, og_ref, gt_ref, cache_ref, cw_ref, cb_ref, bias_ref, ng_ref,
                  C0_ref, n0_ref, m0_ref, y_ref, Co_ref, no_ref, mo_ref,
                  ext_ref, C_ref, n_ref, m_ref, *, H, dk, dv, Lc, W):
    c = pl.program_id(1)
    H8 = SUBLANES

    @pl.when(c == 0)
    def _():
        ext_ref[0:H8, :] = cache_ref[0]
        C_ref[...] = C0_ref[0]
        n_ref[...] = n0_ref[0]
        m_ref[...] = m0_ref[0]

    @pl.when(c > 0)
    def _():
        ext_ref[0:H8, :] = ext_ref[Lc:Lc + H8, :]

    ext_ref[H8:H8 + Lc, :] = qk_ref[0]

    def conv_silu(lo, hi):
        pre = _causal_dwconv(ext_ref[:, lo:hi], cw_ref, slice(lo, hi), Lc) + cb_ref[:, lo:hi]
        return pre * jax.nn.sigmoid(pre)

    gt = gt_ref[0] + bias_ref[...]
    lane = lax.broadcasted_iota(jnp.int32, gt.shape, 1)
    row = lax.broadcasted_iota(jnp.int32, gt.shape, 0)
    b = jax.nn.log_sigmoid(gt)
    sh = 1
    while sh < Lc:
        b = b + jnp.where(row >= sh, pltpu.roll(b, sh, 0), 0.0)
        sh *= 2
    A = jnp.where(lane < H, gt, b)
    AT = A.T
    causal = lax.broadcasted_iota(jnp.int32, (Lc, Lc), 0) >= lax.broadcasted_iota(jnp.int32, (Lc, Lc), 1)

    def stage1(h):
        i_col, b_col = A[:, h:h + 1], A[:, H + h:H + h + 1]
        i_row, b_row = AT[h:h + 1, :], AT[H + h:H + h + 1, :]
        m_prev = m_ref[h:h + 1, 0:1]
        dlog = jnp.where(causal, b_col - b_row + i_row, -jnp.inf)
        inter_log = b_col + m_prev
        m_t = jnp.maximum(inter_log, jnp.max(dlog, -1, keepdims=True))
        dw = jnp.exp(dlog - m_t)
        iw = jnp.exp(inter_log - m_t)
        q = conv_silu(h * dk, (h + 1) * dk)
        k = conv_silu((H + h) * dk, (H + h + 1) * dk) * (dk ** -0.5)
        qb = q.astype(BF16)
        s = _dot_nt(qb, k.astype(BF16)) * dw
        return dict(i_col=i_col, b_col=b_col, m_prev=m_prev, m_t=m_t, iw=iw, q=q, k=k, qb=qb, s=s)

    def stage2(h, t):
        v = v_ref[0, :, h * dv:(h + 1) * dv]
        Ch = C_ref[h]
        nh = n_ref[h:h + 1, :]
        s, iw, m_t, b_col = t['s'], t['iw'], t['m_t'], t['b_col']
        num = _dot(s.astype(BF16), v) + _dot(t['qb'], Ch.astype(BF16)) * iw
        den = jnp.sum(s, -1, keepdims=True) + jnp.sum(t['q'] * nh, -1, keepdims=True) * iw
        hh = num / jnp.maximum(jnp.abs(den), jnp.exp(-m_t))
        m_L, b_L = m_t[Lc - 1:Lc, :], b_col[Lc - 1:Lc, :]
        kw = t['k'] * jnp.exp(b_L - b_col + t['i_col'] - m_L)
        decay = jnp.exp(b_L + t['m_prev'] - m_L)
        C_ref[h] = decay * Ch + _dot_tn(kw.astype(BF16), v)
        n_ref[h:h + 1, :] = decay * nh + jnp.sum(kw, 0, keepdims=True)
        m_ref[h:h + 1, :] = jnp.broadcast_to(m_L, (1, m_ref.shape[1]))
        return hh

    def stage3(h, hh):
        hn = hh * lax.rsqrt(jnp.mean(hh * hh, -1, keepdims=True) + EPS) * ng_ref[:, h * dv:(h + 1) * dv]
        og = og_ref[0, :, h * dv:(h + 1) * dv].astype(F32)
        y_ref[0, :, h * dv:(h + 1) * dv] = (hn * jax.nn.sigmoid(og)).astype(y_ref.dtype)

    t1, t2 = {}, {}
    for step in range(H + 2):
        if step < H:
            t1[step] = stage1(step)
        if 0 <= step - 1 < H:
            t2[step - 1] = stage2(step - 1, t1.pop(step - 1))
        if 0 <= step - 2 < H:
            stage3(step - 2, t2.pop(step - 2))

    @pl.when(c == pl.num_programs(1) - 1)
    def _():
        Co_ref[0] = C_ref[...]
        no_ref[0] = n_ref[...]
        mo_ref[0] = m_ref[...]


def _mlstm(qk, v, og, gt, cache, conv_w, conv_b, gate_bias, norm_g, C0, layer, n0, m0):
    B, L, QK = qk.shape
    _, _, H, dk, dv = C0.shape
    W = conv_w.shape[0]
    NG = gt.shape[-1]
    Lc = min(ML_CHUNK, L)
    assert L % Lc == 0 and Lc >= SUBLANES >= W - 1 and QK == 2 * H * dk and H == SUBLANES and dk == LANES
    cache8 = jnp.pad(cache, ((0, 0), (SUBLANES - (W - 1), 0), (0, 0)))
    m0b = jnp.broadcast_to(m0[:, :, None], (B, H, LANES))
    seq = lambda w: pl.BlockSpec((1, Lc, w), lambda b, c: (b, c, 0))
    const = lambda shape: pl.BlockSpec(shape, lambda b, c: (0,) * len(shape))
    per_b = lambda shape: pl.BlockSpec((1,) + shape, lambda b, c: (b,) + (0,) * len(shape))
    y, C, n, m = pl.pallas_call(
        functools.partial(_mlstm_kernel, H=H, dk=dk, dv=dv, Lc=Lc, W=W),
        grid=(B, L // Lc),
        in_specs=[seq(QK), seq(H * dv), seq(H * dv), seq(NG), per_b((SUBLANES, QK)),
                  const((W, QK)), const((1, QK)), const((1, NG)), const((1, H * dv)),
                  pl.BlockSpec((None, 1, H, dk, dv), lambda b, c: (layer, b, 0, 0, 0)),
                  per_b((H, dk)), per_b((H, LANES))],
        out_specs=[seq(H * dv), per_b((H, dk, dv)), per_b((H, dk)), per_b((H, LANES))],
        out_shape=[jax.ShapeDtypeStruct((B, L, H * dv), BF16), jax.ShapeDtypeStruct(C0.shape[1:], F32),
                   jax.ShapeDtypeStruct(n0.shape, F32), jax.ShapeDtypeStruct((B, H, LANES), F32)],
        scratch_shapes=[pltpu.VMEM((Lc + SUBLANES, QK), F32), pltpu.VMEM((H, dk, dv), F32),
                        pltpu.VMEM((H, dk), F32), pltpu.VMEM((H, LANES), F32)],
        compiler_params=_params(2), name="mlstm",
    )(qk, v, og, gt, cache8, conv_w, conv_b.reshape(1, QK), gate_bias, norm_g.reshape(1, H * dv), C0, n0, m0b)
    return y, C, n, m[:, :, 0]


def _rotary_tables(pos, dk):
    half = dk // 2
    inv = 1.0 / (10000.0 ** jnp.linspace(0.0, 1.0, half, dtype=F32))
    ang = pos.astype(F32)[:, None] * inv[None, :]
    cos, sin = jnp.cos(ang), jnp.sin(ang)
    return jnp.concatenate([cos, cos], -1), jnp.concatenate([-sin, sin], -1)


def _trunk(x, pos0, ret_S, pool_buf, ml_C, ml_n, ml_m, ml_conv, ffn_buf, p):
    B, L, D = x.shape
    depth = p['norm_mix'].shape[0]
    _, _, RH, rdk, rdv = ret_S.shape
    _, _, MH, mdk, mdv = ml_C.shape
    PW = pool_buf.shape[-1]
    qw, rw, mqk, mv = RH * rdk, RH * rdv, MH * mdk, MH * mdv
    assert L >= pool_buf.shape[2] and L >= ml_conv.shape[2]
    assert qw == rw == PW and p['w_in_a'].shape[2] == 2 * qw + 2 * rw + PW
    assert mv == mqk * 2 and p['w_in_c'].shape[2] >= 2 * mqk + 2 * mv + LANES and 2 * MH <= LANES
    cos, sin = _rotary_tables(pos0 + jnp.arange(L, dtype=jnp.int32), rdk)
    sh = lambda t: t.reshape(B, L, -1)
    o_S, o_pool, o_C, o_n, o_m, o_conv, o_ffn = [], [], [], [], [], [], []
    for l in range(depth):
        x2 = x.reshape(B * L, D)
        if l % 2 == 0:
            a = l // 2
            q, k, v, g, u = _proj(x2, p['norm_mix'][l], p['w_in_a'], a, qw, [BF16, BF16, BF16, BF16, F32])
            ret, S = _retention(sh(q), sh(k), sh(v), sh(g), cos, sin, ret_S, a)
            u3 = sh(u)
            pool = _pool(u3, pool_buf[a], p['w_pool'][a], p['pool_scale'][a], pos0)
            acts, w_out, lo = [ret.reshape(B * L, rw), pool.reshape(B * L, PW)], p['w_out_a'], a
            o_S.append(S)
            o_pool.append(u3[:, L - pool_buf.shape[2]:])
        else:
            c = l // 2
            qk, v, og, gt = _proj(x2, p['norm_mix'][l], p['w_in_c'], c, mv, [F32, BF16, BF16], small_cols=LANES)
            gate_bias = jnp.pad(jnp.concatenate([p['b_i'][c], p['b_f'][c]]), (0, LANES - 2 * MH)).reshape(1, LANES)
            qk3 = sh(qk)
            y, C, n, m = _mlstm(qk3, sh(v), sh(og), sh(gt), ml_conv[c], p['conv_c_w'][c], p['conv_c_b'][c],
                                gate_bias, p['ml_norm'][c], ml_C, c, ml_n[c], ml_m[c])
            acts, w_out, lo = [y.reshape(B * L, mv)], p['w_out_c'], c
            o_C.append(C)
            o_n.append(n)
            o_m.append(m)
            o_conv.append(qk3[:, L - ml_conv.shape[2]:])
        x, fb = _ffn(acts, w_out, lo, x, p['norm_ffn'][l], p['w_up'], p['w_gate'], p['ffn_conv_w'],
                     p['ffn_conv_b'], p['w_down'], l, ffn_buf[l],
                     final_g=p['norm_final'] if l == depth - 1 else None)
        o_ffn.append(fb)
    return (x, jnp.stack(o_S), jnp.stack(o_pool), jnp.stack(o_C), jnp.stack(o_n), jnp.stack(o_m),
            jnp.stack(o_conv), jnp.stack(o_ffn))


def kernel(x_prompt, x_sample, state_ret, cache_pool, state_mlstm_C, state_mlstm_n, state_mlstm_m, cache_mlstm_conv, cache_ffn_conv, norm_mix, norm_ffn, norm_final, w_in_a, w_pool, pool_scale, w_out_a, w_in_c, conv_c_w, conv_c_b, b_i, b_f, ml_norm, w_out_c, w_up, w_gate, ffn_conv_w, ffn_conv_b, w_down):
    bf = lambda w: w.astype(BF16)
    gate_pad = -w_in_c.shape[2] % LANES
    w_in_c_p = jnp.pad(bf(w_in_c), ((0, 0), (0, 0), (0, gate_pad)))
    p = dict(norm_mix=norm_mix, norm_ffn=norm_ffn, norm_final=norm_final, w_in_a=bf(w_in_a), w_pool=bf(w_pool),
             pool_scale=pool_scale, w_out_a=bf(w_out_a), w_in_c=w_in_c_p, conv_c_w=conv_c_w, conv_c_b=conv_c_b,
             b_i=b_i, b_f=b_f, ml_norm=ml_norm, w_out_c=bf(w_out_c), w_up=bf(w_up), w_gate=bf(w_gate),
             ffn_conv_w=ffn_conv_w, ffn_conv_b=ffn_conv_b, w_down=bf(w_down))
    B0 = x_prompt.shape[0]
    zeros = lambda a: jnp.zeros((a.shape[0], B0) + a.shape[2:], F32)
    prompt = _trunk(x_prompt, 0, zeros(state_ret), zeros(cache_pool), zeros(state_mlstm_C), zeros(state_mlstm_n),
                    zeros(state_mlstm_m), zeros(cache_mlstm_conv), zeros(cache_ffn_conv), p)
    sample = _trunk(x_sample, PAST_LEN, state_ret, cache_pool, state_mlstm_C, state_mlstm_n, state_mlstm_m,
                    cache_mlstm_conv, cache_ffn_conv, p)
    return (prompt[0], sample[0]) + tuple(prompt[1:]) + tuple(sample[1:])
```
